```python
import math
import jax
import jax.numpy as jnp
from jax import lax
import numpy as np

D_MODEL = 1024
BATCH = 2
SEQ = 8192
DEPTH = 4

GRID_W = 64
CTX_LEN = 256
N_EVEN = (DEPTH + 1) // 2
N_ODD = DEPTH // 2
MIX_W = D_MODEL // 2
GMLP_CHUNK = 128
A_GROUPS = 4
A_GROUP_DIM = MIX_W // A_GROUPS
B_HEADS = 4
B_HEAD_DIM = MIX_W // B_HEADS
B_HALF = B_HEAD_DIM // 2
DIFF_SCALE = B_HALF ** -0.5
Q_BLOCK = 128
C_HEADS = 4
C_DIM = MIX_W // C_HEADS
C_SCALE = C_DIM ** -0.5
D_HEADS = 4
D_DIM = MIX_W // D_HEADS
D_SCALE = D_DIM ** -0.5
SCAN_CHUNK = 64
PEER_HEADS = 8
N_KEYS = 128
N_EXPERTS = N_KEYS * N_KEYS
PK_DIM = 128
PK_TOPK = 16
PEER_TOPK = 16
PEER_BLOCK = 128
ROPE_BASE = 10000.0
EPS = 1e-6

kernel_name = 'hybrid_gmlp_diffattn_hgrn2_retnet_peer_dit'


def rms_norm(x, g):
    xf = x.astype(jnp.float32)
    y = xf * lax.rsqrt(jnp.mean(xf * xf, axis=-1, keepdims=True) + EPS)
    return (y * g.astype(jnp.float32)).astype(x.dtype)


def layer_norm(x, g):
    xf = x.astype(jnp.float32)
    xc = xf - jnp.mean(xf, axis=-1, keepdims=True)
    y = xc * lax.rsqrt(jnp.mean(xc * xc, axis=-1, keepdims=True) + EPS)
    return (y * g.astype(jnp.float32)).astype(x.dtype)


def modulate(h, shift, scale):
    return h * (1.0 + scale) + shift


def to_heads(a, n_heads):
    b, t, _ = a.shape
    return a.reshape(b, t, n_heads, -1).transpose(0, 2, 1, 3)


def from_heads(a):
    b, h, t, d = a.shape
    return a.transpose(0, 2, 1, 3).reshape(b, t, h * d)


def rope_1d(x, pos):
    half = x.shape[-1] // 2
    inv = ROPE_BASE ** (-jnp.arange(half, dtype=jnp.float32) / half)
    ang = pos.astype(jnp.float32)[:, None] * inv[None, :]
    cos, sin = jnp.cos(ang), jnp.sin(ang)
    xf = x.astype(jnp.float32)
    x1, x2 = xf[..., :half], xf[..., half:]
    return jnp.concatenate([x1 * cos - x2 * sin, x1 * sin + x2 * cos], axis=-1).astype(x.dtype)


def rope_2d(x, rows, cols):
    h = x.shape[-1] // 2
    return jnp.concatenate([rope_1d(x[..., :h], rows), rope_1d(x[..., h:], cols)], axis=-1)


def chunk_gmlp(uv, ws, bs, v_gain):
    u, v = jnp.split(jax.nn.gelu(uv), 2, axis=-1)
    v = layer_norm(v, v_gain)
    b, t, _ = v.shape
    vc = v.reshape(b, t // GMLP_CHUNK, GMLP_CHUNK, A_GROUPS, A_GROUP_DIM)
    z = jnp.einsum('gpq,bnqgd->bnpgd', ws, vc) + bs.T[:, :, None]
    return u * z.reshape(b, t, MIX_W)


def diff_attend(q1, q2, k1, k2, v, lam):
    s1 = jnp.einsum('bhqd,bhkd->bhqk', q1, k1).astype(jnp.float32) * DIFF_SCALE
    s2 = jnp.einsum('bhqd,bhkd->bhqk', q2, k2).astype(jnp.float32) * DIFF_SCALE
    p = jax.nn.softmax(s1, axis=-1) - lam * jax.nn.softmax(s2, axis=-1)
    return jnp.einsum('bhqk,bhkd->bhqd', p.astype(v.dtype), v)


def mixer_ab(hx, hy, w_in, ws, bs, v_gain, lam_vecs, sub_gain, lam_init, rows, cols, with_ctx):
    w_kv, w_rest = w_in[:, :2 * MIX_W], w_in[:, 2 * MIX_W:]

    def keys_values(h):
        k, v = jnp.split(h @ w_kv, 2, axis=-1)
        k1, k2 = jnp.split(to_heads(k, B_HEADS), 2, axis=-1)
        return k1, k2, to_heads(v, B_HEADS)

    def gmlp_and_queries(h):
        uv, q = jnp.split(h @ w_rest, [2 * MIX_W], axis=-1)
        q1, q2 = jnp.split(to_heads(q, B_HEADS), 2, axis=-1)
        return uv, q1, q2

    lv = lam_vecs.astype(jnp.float32)
    lam = jnp.exp(jnp.dot(lv[0], lv[1])) - jnp.exp(jnp.dot(lv[2], lv[3])) + lam_init

    def finish(uv, o):
        o = from_heads(rms_norm(o, sub_gain) * (1.0 - lam_init))
        return jnp.concatenate([chunk_gmlp(uv, ws, bs, v_gain), o], axis=-1)

    k1y, k2y, vy = keys_values(hy)
    k1x, k2x, vx = keys_values(hx)
    uvx, q1x, q2x = gmlp_and_queries(hx)
    q1x, q2x, k1x, k2x = (rope_2d(a, rows, cols) for a in (q1x, q2x, k1x, k2x))
    k1a = jnp.concatenate([k1x, k1y], axis=2)
    k2a = jnp.concatenate([k2x, k2y], axis=2)
    va = jnp.concatenate([vx, vy], axis=2)
    b, h, n, _ = q1x.shape
    nb = n // Q_BLOCK

    def blocks(a):
        return jnp.moveaxis(a.reshape(b, h, nb, Q_BLOCK, B_HALF), 2, 0)

    ob = lax.map(lambda qs: diff_attend(qs[0], qs[1], k1a, k2a, va, lam), (blocks(q1x), blocks(q2x)))
    ox = finish(uvx, jnp.moveaxis(ob, 0, 2).reshape(b, h, n, B_HEAD_DIM))
    if not with_ctx:
        return ox, None
    uvy, q1y, q2y = gmlp_and_queries(hy)
    oy = finish(uvy, diff_attend(q1y, q2y, k1y, k2y, vy, lam))
    return ox, oy


def gated_chunk_scan(k, v, log_f, s0, q=None):
    b, h, t, kd = k.shape
    n = t // SCAN_CHUNK

    def chunks(a):
        return jnp.moveaxis(a.reshape(b, h, n, SCAN_CHUNK, a.shape[-1]), 2, 0)

    lower = jnp.tril(jnp.ones((SCAN_CHUNK, SCAN_CHUNK), bool))[:, :, None]
    xs = (chunks(k), chunks(v), chunks(log_f)) + (() if q is None else (chunks(q),))

    def step(s, blk):
        kb = blk[0].astype(jnp.float32)
        vb = blk[1].astype(jnp.float32)
        cum = jnp.cumsum(blk[2].astype(jnp.float32), axis=2)
        end = cum[:, :, -1]
        s_new = jnp.exp(end)[..., None] * s + jnp.einsum('bhck,bhcv->bhkv', kb * jnp.exp(end[:, :, None] - cum), vb)
        if q is None:
            return s_new, None
        qb = blk[3].astype(jnp.float32)
        o = jnp.einsum('bhck,bhkv->bhcv', qb * jnp.exp(cum), s)
        rel = jnp.where(lower, jnp.exp(jnp.minimum(cum[:, :, :, None] - cum[:, :, None], 0.0)), 0.0)
        att = jnp.einsum('bhtk,bhsk,bhtsk->bhts', qb, kb, rel)
        return s_new, o + jnp.einsum('bhts,bhsv->bhtv', att, vb)

    s_end, o = lax.scan(step, s0, xs)
    if q is None:
        return None, s_end
    return jnp.moveaxis(o, 0, 2).reshape(b, h, t, v.shape[-1]).astype(v.dtype), s_end


def bidir_scan(dirs, v, s0_fwd, s0_bwd, q):
    (k_f, lf_f), (k_b, lf_b) = dirs

    def flip(a):
        return jnp.flip(a, axis=2)

    o_f, s_f = gated_chunk_scan(k_f, v, lf_f, s0_fwd, q)
    o_b, s_b = gated_chunk_scan(flip(k_b), flip(v), flip(lf_b), s0_bwd, None if q is None else flip(q))
    o = None if q is None else o_f + flip(o_b)
    return o, s_f, s_b


def mixer_cd(hx, hy, w_in, lb, hg_gain, ret_decay, ret_gain, rows, cols, with_ctx):
    w_state, w_query = w_in[:, :5 * MIX_W], w_in[:, 5 * MIX_W:]
    log_gamma = -jnp.exp(ret_decay.astype(jnp.float32))
    lb_h = lb.reshape(C_HEADS, 1, C_DIM)

    def scan_inputs(h, rotate):
        f_fw, f_bw, i_in, rk, rv = (to_heads(a, C_HEADS) for a in jnp.split(h @ w_state, 5, axis=-1))
        if rotate:
            rk = rope_2d(rk, rows, cols)
        rk = (rk * D_SCALE).astype(jnp.float32)
        b, _, t, _ = rk.shape
        dirs = []
        for d, f_logit in enumerate((f_fw, f_bw)):
            f = lb_h + (1.0 - lb_h) * jax.nn.sigmoid(f_logit.astype(jnp.float32))
            lf_ret = jnp.broadcast_to(log_gamma[d][None, :, None, None], (b, D_HEADS, t, D_DIM))
            dirs.append((jnp.concatenate([1.0 - f, rk], axis=1), jnp.concatenate([jnp.log(f), lf_ret], axis=1)))
        return dirs, jnp.concatenate([i_in, rv], axis=1)

    def queries(h, rotate):
        hq, hg, rq, rg = jnp.split(h @ w_query, 4, axis=-1)
        hq = to_heads(jax.nn.silu(hq), C_HEADS) * C_SCALE
        rq = to_heads(rq, D_HEADS)
        if rotate:
            rq = rope_2d(rq, rows, cols)
        return jnp.concatenate([hq, rq], axis=1), hg, rg

    def outputs(o, hg, rg):
        oh = rms_norm(o[:, :C_HEADS] * jax.nn.sigmoid(to_heads(hg, C_HEADS)), hg_gain)
        orr = from_heads(layer_norm(o[:, C_HEADS:], ret_gain)) * jax.nn.silu(rg)
        return jnp.concatenate([from_heads(oh), orr], axis=-1)

    b = hx.shape[0]
    zeros = jnp.zeros((b, C_HEADS + D_HEADS, C_DIM, D_DIM), jnp.float32)
    dirs_y, v_y = scan_inputs(hy, False)
    q_y, hg_y, rg_y = queries(hy, False) if with_ctx else (None, None, None)
    o_y, s_yf, s_yb = bidir_scan(dirs_y, v_y, zeros, zeros, q_y)
    dirs_x, v_x = scan_inputs(hx, True)
    q_x, hg_x, rg_x = queries(hx, True)
    o_x, _, _ = bidir_scan(dirs_x, v_x, s_yf, s_yb, q_x)
    ox = outputs(o_x, hg_x, rg_x)
    oy = outputs(o_y, hg_y, rg_y) if with_ctx else None
    return ox, oy


def peer(h, wq, sub_keys, u_tab, v_tab):
    b, t, d = h.shape
    z = h.reshape(-1, PEER_BLOCK, d)

    def block(zb):
        q = (zb @ wq).reshape(PEER_BLOCK, PEER_HEADS, 2, PK_DIM)
        s = jnp.einsum('nhpd,hpkd->nhpk', q, sub_keys).astype(jnp.float32)
        sv, si = lax.top_k(s, PK_TOPK)
        cand = (sv[:, :, 0, :, None] + sv[:, :, 1, None, :]).reshape(PEER_BLOCK, PEER_HEADS, PK_TOPK * PK_TOPK)
        cid = (si[:, :, 0, :, None] * N_KEYS + si[:, :, 1, None, :]).reshape(PEER_BLOCK, PEER_HEADS, PK_TOPK * PK_TOPK)
        best, pos = lax.top_k(cand, PEER_TOPK)
        eid = jnp.take_along_axis(cid, pos, axis=-1)
        gate = jax.nn.softmax(best, axis=-1)
        act = jax.nn.gelu(jnp.einsum('nd,nhkd->nhk', zb, u_tab[eid]).astype(jnp.float32))
        return jnp.einsum('nhk,nhkd->nd', (gate * act).astype(zb.dtype), v_tab[eid])

    return lax.map(block, z).reshape(b, t, d)


def setup_inputs(seed: int = 0) -> dict:
    key = jax.random.key(seed)
    ks = jax.random.split(key, 32)
    d = D_MODEL

    def nrm(k, shape, s):
        return jax.random.normal(k, shape, jnp.float32) * s

    base_decay = jnp.log(-jnp.log1p(-(2.0 ** (-5.0 - jnp.arange(D_HEADS, dtype=jnp.float32)))))
    return {
        'x': nrm(ks[0], (BATCH, SEQ, d), 1.0),
        'c': nrm(ks[1], (BATCH, d), 1.0),
        'ctx': nrm(ks[2], (BATCH, CTX_LEN, d), 1.0),
        'c_ctx': nrm(ks[3], (d,), 1.0),
        'w_ada': nrm(ks[4], (DEPTH, d, 6 * d), 0.5 * d ** -0.5),
        'b_ada': nrm(ks[5], (DEPTH, 6 * d), 0.02),
        'norm_g': 1.0 + nrm(ks[6], (DEPTH, 2, d), 0.02),
        'w_in_even': nrm(ks[7], (N_EVEN, d, 5 * MIX_W), d ** -0.5),
        'gmlp_ws': nrm(ks[8], (N_EVEN, A_GROUPS, GMLP_CHUNK, GMLP_CHUNK), GMLP_CHUNK ** -0.5),
        'gmlp_bs': 1.0 + nrm(ks[9], (N_EVEN, A_GROUPS, GMLP_CHUNK), 0.02),
        'gmlp_v_gain': 1.0 + nrm(ks[10], (N_EVEN, MIX_W), 0.02),
        'diff_lambda': nrm(ks[11], (N_EVEN, 4, B_HALF), 0.1),
        'diff_sub_gain': 1.0 + nrm(ks[12], (N_EVEN, B_HEAD_DIM), 0.02),
        'w_in_odd': nrm(ks[13], (N_ODD, d, 9 * MIX_W), d ** -0.5),
        'hgrn_lower_bounds': nrm(ks[14], (DEPTH, MIX_W), 0.5),
        'hgrn_gain': 1.0 + nrm(ks[15], (N_ODD, C_DIM), 0.02),
        'ret_log_decay': base_decay[None, None, :] + nrm(ks[16], (N_ODD, 2, D_HEADS), 0.1),
        'ret_gain': 1.0 + nrm(ks[17], (N_ODD, D_DIM), 0.02),
        'w_out': nrm(ks[18], (DEPTH, 2 * MIX_W, d), (2 * MIX_W) ** -0.5),
        'peer_wq': nrm(ks[19], (DEPTH, d, PEER_HEADS * 2 * PK_DIM), d ** -0.5),
        'peer_sub_keys': nrm(ks[20], (DEPTH, PEER_HEADS, 2, N_KEYS, PK_DIM), PK_DIM ** -0.5),
        'peer_u': nrm(ks[21], (DEPTH, N_EXPERTS, d), d ** -0.5),
        'peer_v': nrm(ks[22], (DEPTH, N_EXPERTS, d), 0.5),
        'final_g': 1.0 + nrm(ks[23], (d,), 0.02),
    }


def reference(x, c, ctx, c_ctx, w_ada, b_ada, norm_g, w_in_even, gmlp_ws, gmlp_bs,
              gmlp_v_gain, diff_lambda, diff_sub_gain, w_in_odd, hgrn_lower_bounds,
              hgrn_gain, ret_log_decay, ret_gain, w_out, peer_wq, peer_sub_keys,
              peer_u, peer_v, final_g):
    n_tok = x.shape[1]
    n_rows = n_tok // GRID_W
    rows = jnp.repeat(jnp.arange(n_rows), GRID_W)
    cols = jnp.tile(jnp.arange(GRID_W), n_rows)
    lb_all = jnp.cumsum(jax.nn.softmax(hgrn_lower_bounds.astype(jnp.float32), axis=0), axis=0)
    lb_all = lb_all - lb_all[0]
    y = ctx
    n_ctx = ctx.shape[1]
    for i in range(DEPTH):
        last = i == DEPTH - 1
        j = i // 2
        mod_x = jax.nn.silu(c) @ w_ada[i] + b_ada[i]
        mod_y = jax.nn.silu(c_ctx) @ w_ada[i] + b_ada[i]
        sh1, sc1, g1, sh2, sc2, g2 = jnp.split(mod_x[:, None, :], 6, axis=-1)
        csh1, csc1, cg1, csh2, csc2, cg2 = jnp.split(mod_y, 6, axis=-1)
        hx = modulate(rms_norm(x, norm_g[i, 0]), sh1, sc1)
        hy = modulate(rms_norm(y, norm_g[i, 0]), csh1, csc1)
        if i % 2 == 0:
            lam_init = 0.8 - 0.6 * math.exp(-0.3 * i)
            ox, oy = mixer_ab(hx, hy, w_in_even[j], gmlp_ws[j], gmlp_bs[j], gmlp_v_gain[j],
                              diff_lambda[j], diff_sub_gain[j], lam_init, rows, cols, not last)
        else:
            ox, oy = mixer_cd(hx, hy, w_in_odd[j], lb_all[i], hgrn_gain[j], ret_log_decay[j],
                              ret_gain[j], rows, cols, not last)
        x = x + g1 * (ox @ w_out[i])
        hx2 = modulate(rms_norm(x, norm_g[i, 1]), sh2, sc2)
        if last:
            x = x + g2 * peer(hx2, peer_wq[i], peer_sub_keys[i], peer_u[i], peer_v[i])
        else:
            y = y + cg1 * (oy @ w_out[i])
            hy2 = modulate(rms_norm(y, norm_g[i, 1]), csh2, csc2)
            p = peer(jnp.concatenate([hy2, hx2], axis=1), peer_wq[i], peer_sub_keys[i], peer_u[i], peer_v[i])
            y = y + cg2 * p[:, :n_ctx]
            x = x + g2 * p[:, n_ctx:]
    return rms_norm(x, final_g)
```

```python
import functools
import math

import jax
import jax.numpy as jnp
from jax import lax
from jax.experimental import pallas as pl
from jax.experimental.pallas import tpu as pltpu

F32 = jnp.float32
BF16 = jnp.bfloat16

D_MODEL = 1024
MIX_W = D_MODEL // 2
HEAD_W = 128
N_HEADS = MIX_W // HEAD_W
GRID_W = 64
GMLP_CHUNK = 128
SCAN_CHUNK = 64
PEER_HEADS = 8
N_KEYS = 128
PK_TOPK = 16
ROPE_BASE = 10000.0
EPS = 1e-6
DIFF_SCALE = 64 ** -0.5
C_SCALE = HEAD_W ** -0.5
D_SCALE = HEAD_W ** -0.5

TOK_BLOCK = 256
PEER_TOK = 512
PEER_EXP = 1024
PEER_ROWS = PEER_EXP // N_KEYS
VMEM_LIMIT = 48 * 1024 * 1024

NT_DIMS = (((1,), (1,)), ((), ()))
TN_DIMS = (((0,), (0,)), ((), ()))


def _cparams(sem):
    return pltpu.CompilerParams(dimension_semantics=sem, vmem_limit_bytes=VMEM_LIMIT)


def _gelu(x):
    return 0.5 * x * (1.0 + jnp.tanh(0.7978845608028654 * (x + 0.044715 * x * x * x)))


def _sigmoid(x):
    return 1.0 / (1.0 + jnp.exp(-x))


def _mod_row(which, n_batch):
    def index(b, t):
        return (jnp.where(t == 0, n_batch, b) * 6 + which, 0, 0)
    return index


def _ada_kernel(c_ref, w_ref, b_ref, o_ref):
    c = c_ref[...]
    a = (c * _sigmoid(c)).astype(BF16)
    o_ref[0] = jnp.dot(a, w_ref[0].astype(BF16), preferred_element_type=F32) + b_ref[0]


def _ada_mod(cvec, w_ada, b_ada):
    depth, d, n = w_ada.shape
    tn = 1536
    return pl.pallas_call(
        _ada_kernel,
        grid=(depth, n // tn),
        in_specs=[
            pl.BlockSpec((8, d), lambda i, j: (0, 0)),
            pl.BlockSpec((1, d, tn), lambda i, j: (i, 0, j)),
            pl.BlockSpec((1, 1, tn), lambda i, j: (i, 0, j)),
        ],
        out_specs=pl.BlockSpec((1, 8, tn), lambda i, j: (i, 0, j)),
        out_shape=jax.ShapeDtypeStruct((depth, 8, n), F32),
        compiler_params=_cparams(("parallel", "parallel")),
    )(cvec, w_ada, b_ada.reshape(depth, 1, n))


def _rope_slab(seg, cos, sin, shift):
    lane = lax.broadcasted_iota(jnp.int32, seg.shape, 1)
    first = (lane % (2 * shift)) < shift
    partner = jnp.where(first, pltpu.roll(seg, HEAD_W - shift, 1), pltpu.roll(seg, shift, 1))
    return seg * cos + partner * sin


def _proj_kernel(*refs, rope_slabs, rope_shift, add_peer):
    if add_peer:
        x_ref, p_ref, g2_ref, gn_ref, sh_ref, sc_ref, w_ref, cos_ref, sin_ref, o_ref, s_ref = refs
        x = x_ref[0] + g2_ref[0] * p_ref[0]
        s_ref[0] = x
    else:
        x_ref, gn_ref, sh_ref, sc_ref, w_ref, cos_ref, sin_ref, o_ref = refs
        x = x_ref[0]
    h = x * lax.rsqrt(jnp.mean(x * x, axis=-1, keepdims=True) + EPS) * gn_ref[...]
    h = h * (1.0 + sc_ref[0]) + sh_ref[0]
    acc = jnp.dot(h.astype(BF16), w_ref[...], preferred_element_type=F32)
    o_ref[0] = acc.astype(o_ref.dtype)
    cos = cos_ref[...]
    sin = sin_ref[...]
    for c in rope_slabs:
        seg = acc[:, c * HEAD_W:(c + 1) * HEAD_W]
        o_ref[0, :, c * HEAD_W:(c + 1) * HEAD_W] = _rope_slab(seg, cos, sin, rope_shift).astype(o_ref.dtype)


def _proj(x, peer, prev_mod, mod, gn, w, cos, sin, *, rope_slabs, rope_shift, out_dtype):
    nb, t, d = x.shape
    n = w.shape[1]
    blk = pl.BlockSpec((1, TOK_BLOCK, d), lambda b, i: (b, i, 0))
    vec = lambda which: pl.BlockSpec((1, 1, d), _mod_row(which, nb))
    in_specs = [blk]
    args = [x]
    if peer is not None:
        in_specs += [blk, vec(5)]
        args += [peer, prev_mod]
    in_specs += [
        pl.BlockSpec((1, d), lambda b, i: (0, 0)),
        vec(0), vec(1),
        pl.BlockSpec((d, n), lambda b, i: (0, 0)),
        pl.BlockSpec((TOK_BLOCK, HEAD_W), lambda b, i: (i, 0)),
        pl.BlockSpec((TOK_BLOCK, HEAD_W), lambda b, i: (i, 0)),
    ]
    args += [gn.reshape(1, d), mod, mod, w, cos, sin]
    out_specs = [pl.BlockSpec((1, TOK_BLOCK, n), lambda b, i: (b, i, 0))]
    out_shape = [jax.ShapeDtypeStruct((nb, t, n), out_dtype)]
    if peer is not None:
        out_specs.append(blk)
        out_shape.append(jax.ShapeDtypeStruct((nb, t, d), F32))
    res = pl.pallas_call(
        functools.partial(_proj_kernel, rope_slabs=rope_slabs, rope_shift=rope_shift,
                          add_peer=peer is not None),
        grid=(nb, t // TOK_BLOCK),
        in_specs=in_specs,
        out_specs=out_specs,
        out_shape=out_shape,
        compiler_params=_cparams(("parallel", "parallel")),
    )(*args)
    return (res[0], res[1]) if peer is not None else (res[0], x)


def _attn_kernel(lv_ref, q_ref, k_ref, v_ref, sg_ref, o_ref, *, n_kv, lam_init):
    qi = pl.program_id(2)
    q = q_ref[0]
    lane = lax.broadcasted_iota(jnp.int32, q.shape, 1)
    qs = q * jnp.asarray(DIFF_SCALE, q.dtype)
    zero = jnp.zeros_like(qs)
    q1 = jnp.where(lane < HEAD_W // 2, qs, zero)
    q2 = jnp.where(lane < HEAD_W // 2, zero, qs)
    tq = q.shape[0]

    def step(c, carry):
        m1, l1, a1, m2, l2, a2 = carry
        off = pl.multiple_of(c * TOK_BLOCK, TOK_BLOCK)
        k = k_ref[0, pl.ds(off, TOK_BLOCK), :]
        v = v_ref[0, pl.ds(off, TOK_BLOCK), :]

        def online(qh, m, l, a):
            s = lax.dot_general(qh, k, NT_DIMS, preferred_element_type=F32)
            m_new = jnp.maximum(m, jnp.max(s, axis=-1, keepdims=True))
            alpha = jnp.exp(m - m_new)
            p = jnp.exp(s - m_new)
            l_new = alpha * l + jnp.sum(p, axis=-1, keepdims=True)
            a_new = alpha * a + jnp.dot(p.astype(v.dtype), v, preferred_element_type=F32)
            return m_new, l_new, a_new

        m1, l1, a1 = online(q1, m1, l1, a1)
        m2, l2, a2 = online(q2, m2, l2, a2)
        return m1, l1, a1, m2, l2, a2

    neg = jnp.full((tq, 1), -jnp.inf, F32)
    z1 = jnp.zeros((tq, 1), F32)
    za = jnp.zeros((tq, HEAD_W), F32)
    n_steps = jnp.where(qi == 0, 1, n_kv)
    m1, l1, a1, m2, l2, a2 = lax.fori_loop(0, n_steps, step, (neg, z1, za, neg, z1, za))

    lv = lv_ref[...]
    lam = (jnp.exp(jnp.sum(lv[0:1] * lv[1:2], axis=-1, keepdims=True))
           - jnp.exp(jnp.sum(lv[2:3] * lv[3:4], axis=-1, keepdims=True)) + lam_init)
    o = a1 / l1 - lam * (a2 / l2)
    o = o * lax.rsqrt(jnp.mean(o * o, axis=-1, keepdims=True) + EPS) * sg_ref[...]
    o_ref[0] = (o * (1.0 - lam_init)).astype(o_ref.dtype)


def _diff_attention(proj, lam_vecs, sub_gain, lam_init):
    nb, t, _ = proj.shape
    n_q = t // TOK_BLOCK
    return pl.pallas_call(
        functools.partial(_attn_kernel, n_kv=n_q, lam_init=lam_init),
        grid=(nb, N_HEADS, n_q),
        in_specs=[
            pl.BlockSpec((4, HEAD_W // 2), lambda b, h, i: (0, 0)),
            pl.BlockSpec((1, TOK_BLOCK, HEAD_W), lambda b, h, i: (b, i, 4 * N_HEADS + h)),
            pl.BlockSpec((1, t, HEAD_W), lambda b, h, i: (b, 0, h)),
            pl.BlockSpec((1, t, HEAD_W), lambda b, h, i: (b, 0, N_HEADS + h)),
            pl.BlockSpec((1, HEAD_W), lambda b, h, i: (0, 0)),
        ],
        out_specs=pl.BlockSpec((1, TOK_BLOCK, HEAD_W), lambda b, h, i: (b, i, h)),
        out_shape=jax.ShapeDtypeStruct((nb, t, MIX_W), BF16),
        compiler_params=_cparams(("parallel", "parallel", "arbitrary")),
    )(lam_vecs, proj, proj, proj, sub_gain.reshape(1, HEAD_W))


def _gmlp_kernel(u_ref, v_ref, ws_ref, bs_ref, vg_ref, o_ref):
    u = _gelu(u_ref[0].astype(F32))
    v = _gelu(v_ref[0].astype(F32))
    vc = v - jnp.mean(v, axis=-1, keepdims=True)
    vn = vc * lax.rsqrt(jnp.mean(vc * vc, axis=-1, keepdims=True) + EPS) * vg_ref[...]
    vn = vn.astype(BF16)
    for ch in range(TOK_BLOCK // GMLP_CHUNK):
        rows = slice(ch * GMLP_CHUNK, (ch + 1) * GMLP_CHUNK)
        for g in range(N_HEADS):
            cols = slice(g * HEAD_W, (g + 1) * HEAD_W)
            z = jnp.dot(ws_ref[g], vn[rows, cols], preferred_element_type=F32) + bs_ref[g]
            o_ref[0, rows, cols] = (u[rows, cols] * z).astype(o_ref.dtype)


def _gmlp(proj, ws, bs, v_gain):
    nb, t, _ = proj.shape
    bs_b = jnp.broadcast_to(bs[:, :, None], (N_HEADS, GMLP_CHUNK, HEAD_W)).astype(F32)
    return pl.pallas_call(
        _gmlp_kernel,
        grid=(nb, t // TOK_BLOCK),
        in_specs=[
            pl.BlockSpec((1, TOK_BLOCK, MIX_W), lambda b, i: (b, i, 2)),
            pl.BlockSpec((1, TOK_BLOCK, MIX_W), lambda b, i: (b, i, 3)),
            pl.BlockSpec((N_HEADS, GMLP_CHUNK, GMLP_CHUNK), lambda b, i: (0, 0, 0)),
            pl.BlockSpec((N_HEADS, GMLP_CHUNK, HEAD_W), lambda b, i: (0, 0, 0)),
            pl.BlockSpec((1, MIX_W), lambda b, i: (0, 0)),
        ],
        out_specs=pl.BlockSpec((1, TOK_BLOCK, MIX_W), lambda b, i: (b, i, 0)),
        out_shape=jax.ShapeDtypeStruct((nb, t, MIX_W), BF16),
        compiler_params=_cparams(("parallel", "parallel")),
    )(proj, proj, ws.astype(BF16), bs_b, v_gain.reshape(1, MIX_W))


def _scan_block_index(n_blk):
    def blk(d, t):
        return jnp.where(d == 0, t, jnp.where(t == 0, 0, n_blk - t))
    return blk


def _scan_chunk(q, k, v, logf, st_ref, d):
    c = q.shape[0]
    sign = 1 - 2 * d
    row = lax.broadcasted_iota(jnp.int32, (c, c), 0)
    col = lax.broadcasted_iota(jnp.int32, (c, c), 1)
    tri = ((row - col) * sign >= 0).astype(F32)
    cum = jnp.dot(tri, logf, preferred_element_type=F32, precision=lax.Precision.HIGHEST)
    tot = jnp.sum(logf, axis=0, keepdims=True)
    st = st_ref[...]
    o = lax.dot_general(q * jnp.exp(cum), st, NT_DIMS, preferred_element_type=F32)
    t_idx = lax.broadcasted_iota(jnp.int32, (c, 1), 0)
    for s in range(c):
        w = q * jnp.exp(jnp.minimum(cum - cum[s:s + 1], 0.0)) * k[s:s + 1]
        a = jnp.sum(w, axis=-1, keepdims=True)
        o = o + jnp.where((t_idx - s) * sign >= 0, a, 0.0) * v[s:s + 1]
    kd = k * jnp.exp(tot - cum)
    st_ref[...] = st * jnp.exp(tot) + lax.dot_general(v, kd, TN_DIMS, preferred_element_type=F32)
    return o


def _hgrn_kernel(fl_ref, i_ref, hq_ref, lb_ref, o_ref, st_ref):
    d = pl.program_id(2)

    @pl.when(pl.program_id(3) == 0)
    def _():
        st_ref[...] = jnp.zeros_like(st_ref)

    lb = lb_ref[0]

    def chunk(j, carry):
        ci = jnp.where(d == 0, j, TOK_BLOCK // SCAN_CHUNK - 1 - j)
        rows = pl.ds(pl.multiple_of(ci * SCAN_CHUNK, SCAN_CHUNK), SCAN_CHUNK)
        f = lb + (1.0 - lb) * _sigmoid(fl_ref[0, rows, :])
        hq = hq_ref[0, rows, :]
        q = hq * _sigmoid(hq) * C_SCALE
        o_ref[0, 0, rows, :] = _scan_chunk(q, 1.0 - f, i_ref[0, rows, :], jnp.log(f), st_ref, d)
        return carry

    lax.fori_loop(0, TOK_BLOCK // SCAN_CHUNK, chunk, 0)


def _ret_kernel(rk_ref, rv_ref, rq_ref, dec_ref, o_ref, st_ref):
    d = pl.program_id(2)

    @pl.when(pl.program_id(3) == 0)
    def _():
        st_ref[...] = jnp.zeros_like(st_ref)

    c = SCAN_CHUNK
    lg = -jnp.exp(dec_ref[0])
    row = lax.broadcasted_iota(jnp.int32, (c, c), 0)
    col = lax.broadcasted_iota(jnp.int32, (c, c), 1)
    lag = (row - col) * (1 - 2 * d)
    decay = jnp.where(lag >= 0, jnp.exp(lag.astype(F32) * lg[:, :c]), 0.0)
    t_idx = lax.broadcasted_iota(jnp.int32, (c, 1), 0)
    steps = jnp.where(d == 0, t_idx + 1, c - t_idx)
    cum = steps.astype(F32) * lg
    tot = c * lg

    def chunk(j, carry):
        ci = jnp.where(d == 0, j, TOK_BLOCK // SCAN_CHUNK - 1 - j)
        rows = pl.ds(pl.multiple_of(ci * SCAN_CHUNK, SCAN_CHUNK), SCAN_CHUNK)
        q = rq_ref[0, rows, :]
        k = rk_ref[0, rows, :] * D_SCALE
        v = rv_ref[0, rows, :]
        st = st_ref[...]
        o = lax.dot_general(q * jnp.exp(cum), st, NT_DIMS, preferred_element_type=F32)
        att = lax.dot_general(q, k, NT_DIMS, preferred_element_type=F32) * decay
        o_ref[0, 0, rows, :] = o + jnp.dot(att, v, preferred_element_type=F32)
        kd = k * jnp.exp(tot - cum)
        st_ref[...] = st * jnp.exp(tot) + lax.dot_general(v, kd, TN_DIMS, preferred_element_type=F32)
        return carry

    lax.fori_loop(0, TOK_BLOCK // SCAN_CHUNK, chunk, 0)


def _scans(proj, lb, ret_decay):
    nb, t, _ = proj.shape
    n_blk = t // TOK_BLOCK
    blk = _scan_block_index(n_blk)
    col = lambda base, per_dir: pl.BlockSpec(
        (1, TOK_BLOCK, HEAD_W), lambda b, h, d, i: (b, blk(d, i), base + per_dir * N_HEADS * d + h))
    out_spec = pl.BlockSpec((1, 1, TOK_BLOCK, HEAD_W), lambda b, h, d, i: (b, d, blk(d, i), h))
    out_shape = jax.ShapeDtypeStruct((nb, 2, t, MIX_W), F32)
    grid = (nb, N_HEADS, 2, n_blk)
    sem = ("parallel", "parallel", "parallel", "arbitrary")
    state = pltpu.VMEM((HEAD_W, HEAD_W), F32)
    o_h = pl.pallas_call(
        _hgrn_kernel, grid=grid,
        in_specs=[col(0, 1), col(8, 0), col(20, 0),
                  pl.BlockSpec((1, 1, HEAD_W), lambda b, h, d, i: (h, 0, 0))],
        out_specs=out_spec, out_shape=out_shape,
        scratch_shapes=[state],
        compiler_params=_cparams(sem),
    )(proj, proj, proj, lb.reshape(N_HEADS, 1, HEAD_W))
    dec = jnp.broadcast_to(ret_decay.reshape(2 * N_HEADS, 1, 1), (2 * N_HEADS, 1, HEAD_W)).astype(F32)
    o_r = pl.pallas_call(
        _ret_kernel, grid=grid,
        in_specs=[col(12, 0), col(16, 0), col(28, 0),
                  pl.BlockSpec((1, 1, HEAD_W), lambda b, h, d, i: (d * N_HEADS + h, 0, 0))],
        out_specs=out_spec, out_shape=out_shape,
        scratch_shapes=[state],
        compiler_params=_cparams(sem),
    )(proj, proj, proj, dec)
    return o_h, o_r


def _odd_finish_kernel(oh_ref, or_ref, hg_ref, rg_ref, hgain_ref, rgain_ref, a_ref, b_ref):
    oh = oh_ref[0, 0] + oh_ref[0, 1]
    orr = or_ref[0, 0] + or_ref[0, 1]
    gh = oh * _sigmoid(hg_ref[0])
    rg = rg_ref[0]
    for h in range(N_HEADS):
        cols = slice(h * HEAD_W, (h + 1) * HEAD_W)
        x = gh[:, cols]
        a_ref[0, :, cols] = (x * lax.rsqrt(jnp.mean(x * x, axis=-1, keepdims=True) + EPS)
                             * hgain_ref[...]).astype(a_ref.dtype)
        y = orr[:, cols]
        yc = y - jnp.mean(y, axis=-1, keepdims=True)
        yn = yc * lax.rsqrt(jnp.mean(yc * yc, axis=-1, keepdims=True) + EPS) * rgain_ref[...]
        r = rg[:, cols]
        b_ref[0, :, cols] = (yn * (r * _sigmoid(r))).astype(b_ref.dtype)


def _odd_finish(o_h, o_r, proj, hg_gain, ret_gain):
    nb, _, t, _ = o_h.shape
    o_spec = pl.BlockSpec((1, 2, TOK_BLOCK, MIX_W), lambda b, i: (b, 0, i, 0))
    out_spec = pl.BlockSpec((1, TOK_BLOCK, MIX_W), lambda b, i: (b, i, 0))
    return pl.pallas_call(
        _odd_finish_kernel,
        grid=(nb, t // TOK_BLOCK),
        in_specs=[o_spec, o_spec,
                  pl.BlockSpec((1, TOK_BLOCK, MIX_W), lambda b, i: (b, i, 6)),
                  pl.BlockSpec((1, TOK_BLOCK, MIX_W), lambda b, i: (b, i, 8)),
                  pl.BlockSpec((1, HEAD_W), lambda b, i: (0, 0)),
                  pl.BlockSpec((1, HEAD_W), lambda b, i: (0, 0))],
        out_specs=[out_spec, out_spec],
        out_shape=[jax.ShapeDtypeStruct((nb, t, MIX_W), BF16)] * 2,
        compiler_params=_cparams(("parallel", "parallel")),
    )(o_h, o_r, proj, proj, hg_gain.reshape(1, HEAD_W), ret_gain.reshape(1, HEAD_W))


def _out_kernel(x_ref, a_ref, b_ref, wa_ref, wb_ref, g1_ref, gn_ref, sh_ref, sc_ref, wq_ref,
                xo_ref, h_ref, q_ref):
    y = (jnp.dot(a_ref[0], wa_ref[...], preferred_element_type=F32)
         + jnp.dot(b_ref[0], wb_ref[...], preferred_element_type=F32))
    x = x_ref[0] + g1_ref[0] * y
    xo_ref[0] = x
    h = x * lax.rsqrt(jnp.mean(x * x, axis=-1, keepdims=True) + EPS) * gn_ref[...]
    h = (h * (1.0 + sc_ref[0]) + sh_ref[0]).astype(BF16)
    h_ref[0] = h
    q_ref[0] = jnp.dot(h, wq_ref[...], preferred_element_type=F32).astype(q_ref.dtype)


def _out_proj(x, a, b, w_out, mod, gn, wq):
    nb, t, d = x.shape
    nq = wq.shape[1]
    blk = lambda w: pl.BlockSpec((1, TOK_BLOCK, w), lambda bb, i: (bb, i, 0))
    vec = lambda which: pl.BlockSpec((1, 1, d), _mod_row(which, nb))
    full = lambda r, c: pl.BlockSpec((r, c), lambda bb, i: (0, 0))
    w16 = w_out.astype(BF16)
    return pl.pallas_call(
        _out_kernel,
        grid=(nb, t // TOK_BLOCK),
        in_specs=[blk(d), blk(MIX_W), blk(MIX_W), full(MIX_W, d), full(MIX_W, d),
                  vec(2), full(1, d), vec(3), vec(4), full(d, nq)],
        out_specs=[blk(d), blk(d), blk(nq)],
        out_shape=[jax.ShapeDtypeStruct((nb, t, d), F32),
                   jax.ShapeDtypeStruct((nb, t, d), BF16),
                   jax.ShapeDtypeStruct((nb, t, nq), BF16)],
        compiler_params=_cparams(("parallel", "parallel")),
    )(x, a, b, w16[:MIX_W], w16[MIX_W:], mod, gn.reshape(1, d), mod, mod, wq.astype(BF16))


def _top_values(work, n):
    rows = []
    for _ in range(n):
        m = jnp.max(work, axis=0, keepdims=True)
        rows.append(m)
        work = jnp.where(work == m, -jnp.inf, work)
    return rows


def _peer_kernel(h_ref, q_ref, keys_ref, u_ref, vt_ref, o_ref,
                 s0_ref, s1_ref, e0_ref, e1_ref, tau_ref, top_ref, act_ref, w_ref, acc_ref):
    ec = pl.program_id(1)
    n_tok = h_ref.shape[0]

    @pl.when(ec == 0)
    def _():
        acc_ref[...] = jnp.zeros_like(acc_ref)
        for h in range(PEER_HEADS):
            tops = []
            for side, s_ref in ((0, s0_ref), (1, s1_ref)):
                slab = (2 * h + side) * N_KEYS
                s = lax.dot_general(keys_ref[h, side], q_ref[:, slab:slab + N_KEYS], NT_DIMS,
                                    preferred_element_type=F32)
                s_ref[h] = s
                vals = _top_values(s, PK_TOPK)
                for r in range(PK_TOPK):
                    top_ref[side, r:r + 1, :] = vals[r]
                tops.append(vals)
            t0, t1 = tops
            first16 = top_ref[1]
            cands = [t0[0] + first16]
            cands += [t0[a] + first16[:8] for a in range(1, 8)]
            cands += [top_ref[0, 8:16, :] + t1[0]]
            best = _top_values(jnp.concatenate(cands, axis=0), PK_TOPK)
            z = jnp.zeros_like(best[0])
            for r in range(PK_TOPK):
                z = z + jnp.exp(best[r] - best[0])
            tau_ref[h] = jnp.broadcast_to(best[PK_TOPK - 1], tau_ref.shape[1:])
            e0_ref[h] = jnp.exp(s0_ref[h] - t0[0])
            e1_ref[h] = jnp.exp(s1_ref[h] - t1[0]) / z

    act_ref[...] = lax.dot_general(u_ref[...], h_ref[...], NT_DIMS, preferred_element_type=F32)
    first_key = pl.multiple_of(ec * PEER_ROWS, PEER_ROWS)

    def tile(ct, carry):
        lanes = pl.ds(pl.multiple_of(ct * N_KEYS, N_KEYS), N_KEYS)
        for il in range(PEER_ROWS):
            rows = slice(il * N_KEYS, (il + 1) * N_KEYS)
            g = jnp.zeros((N_KEYS, N_KEYS), F32)
            for h in range(PEER_HEADS):
                s0 = s0_ref[h, pl.ds(first_key, PEER_ROWS), lanes]
                e0 = e0_ref[h, pl.ds(first_key, PEER_ROWS), lanes]
                pair = s0[il:il + 1] + s1_ref[h, :, lanes]
                gate = e0[il:il + 1] * e1_ref[h, :, lanes]
                g = g + jnp.where(pair >= tau_ref[h, 0:1, lanes], gate, 0.0)
            w_ref[rows, lanes] = (g * _gelu(act_ref[rows, lanes])).astype(w_ref.dtype)
        return carry

    lax.fori_loop(0, n_tok // N_KEYS, tile, 0)
    acc_ref[...] += jnp.dot(vt_ref[...], w_ref[...], preferred_element_type=F32)

    @pl.when(ec == pl.num_programs(1) - 1)
    def _():
        o_ref[...] = acc_ref[...].T


def _peer(h2, q, sub_keys, u_tab, v_tab):
    n, d = h2.shape
    n_exp = u_tab.shape[0]
    nq = q.shape[1]
    heads_tile = lambda: pltpu.VMEM((PEER_HEADS, N_KEYS, PEER_TOK), F32)
    return pl.pallas_call(
        _peer_kernel,
        grid=(n // PEER_TOK, n_exp // PEER_EXP),
        in_specs=[
            pl.BlockSpec((PEER_TOK, d), lambda t, e: (t, 0)),
            pl.BlockSpec((PEER_TOK, nq), lambda t, e: (t, 0)),
            pl.BlockSpec((PEER_HEADS, 2, N_KEYS, N_KEYS), lambda t, e: (0, 0, 0, 0)),
            pl.BlockSpec((PEER_EXP, d), lambda t, e: (e, 0)),
            pl.BlockSpec((d, PEER_EXP), lambda t, e: (0, e)),
        ],
        out_specs=pl.BlockSpec((PEER_TOK, d), lambda t, e: (t, 0)),
        out_shape=jax.ShapeDtypeStruct((n, d), F32),
        scratch_shapes=[heads_tile(), heads_tile(), heads_tile(), heads_tile(),
                        pltpu.VMEM((PEER_HEADS, 8, PEER_TOK), F32),
                        pltpu.VMEM((2, PK_TOPK, PEER_TOK), F32),
                        pltpu.VMEM((PEER_EXP, PEER_TOK), F32),
                        pltpu.VMEM((PEER_EXP, PEER_TOK), BF16),
                        pltpu.VMEM((d, PEER_TOK), F32)],
        compiler_params=_cparams(("parallel", "arbitrary")),
    )(h2, q, sub_keys.astype(BF16), u_tab.astype(BF16), v_tab.astype(BF16).T)


def _final_kernel(x_ref, p_ref, g2_ref, g_ref, o_ref):
    x = x_ref[0] + g2_ref[0] * p_ref[0]
    o_ref[0] = x * lax.rsqrt(jnp.mean(x * x, axis=-1, keepdims=True) + EPS) * g_ref[...]


def _final_norm(x, peer, mod, final_g, n_ctx):
    nb, t, d = x.shape
    skip = n_ctx // TOK_BLOCK
    blk = pl.BlockSpec((1, TOK_BLOCK, d), lambda b, i: (b, i + skip, 0))
    return pl.pallas_call(
        _final_kernel,
        grid=(nb, (t - n_ctx) // TOK_BLOCK),
        in_specs=[blk, blk,
                  pl.BlockSpec((1, 1, d), lambda b, i: (b * 6 + 5, 0, 0)),
                  pl.BlockSpec((1, d), lambda b, i: (0, 0))],
        out_specs=pl.BlockSpec((1, TOK_BLOCK, d), lambda b, i: (b, i, 0)),
        out_shape=jax.ShapeDtypeStruct((nb, t - n_ctx, d), F32),
        compiler_params=_cparams(("parallel", "parallel")),
    )(x, peer, mod, final_g.reshape(1, d))


def _rope_tables(n_ctx, n_tok, half):
    lane = jnp.arange(HEAD_W)
    part_w = 2 * half
    inv = ROPE_BASE ** (-(lane % half).astype(F32) / half)
    use_col = (lane // part_w) % 2 == 1
    tok = jnp.arange(n_tok)
    pos = jnp.where(use_col[None, :], (tok % GRID_W)[:, None], (tok // GRID_W)[:, None]).astype(F32)
    ang = pos * inv[None, :]
    sign = jnp.where((lane % part_w) < half, -1.0, 1.0)
    cos = jnp.concatenate([jnp.ones((n_ctx, HEAD_W), F32), jnp.cos(ang)], axis=0)
    sin = jnp.concatenate([jnp.zeros((n_ctx, HEAD_W), F32), jnp.sin(ang) * sign[None, :]], axis=0)
    return cos, sin


def kernel(x, c, ctx, c_ctx, w_ada, b_ada, norm_g, w_in_even, gmlp_ws, gmlp_bs, gmlp_v_gain, diff_lambda, diff_sub_gain, w_in_odd, hgrn_lower_bounds, hgrn_gain, ret_log_decay, ret_gain, w_out, peer_wq, peer_sub_keys, peer_u, peer_v, final_g):
    nb, n_tok, d = x.shape
    n_ctx = ctx.shape[1]
    depth = w_ada.shape[0]
    assert d == D_MODEL and n_ctx == TOK_BLOCK and n_tok % TOK_BLOCK == 0 and nb < 8
    assert (nb * (n_ctx + n_tok)) % PEER_TOK == 0
    t = n_ctx + n_tok

    stream = jnp.concatenate([ctx, x], axis=1)
    cvec = jnp.zeros((8, d), F32).at[:nb].set(c).at[nb].set(c_ctx)
    mod_all = _ada_mod(cvec, w_ada, b_ada).reshape(depth, 8 * 6, 1, d)

    lb_all = jnp.cumsum(jax.nn.softmax(hgrn_lower_bounds.astype(F32), axis=0), axis=0)
    lb_all = lb_all - lb_all[0]
    cos_e, sin_e = _rope_tables(n_ctx, n_tok, 16)
    cos_o, sin_o = _rope_tables(n_ctx, n_tok, 32)

    peer_out = None
    prev_mod = None
    for i in range(depth):
        j = i // 2
        mod = mod_all[i]
        if i % 2 == 0:
            lam_init = 0.8 - 0.6 * math.exp(-0.3 * i)
            proj, stream = _proj(stream, peer_out, prev_mod, mod, norm_g[i, 0], w_in_even[j].astype(BF16),
                                 cos_e, sin_e, rope_slabs=tuple(range(0, 4)) + tuple(range(16, 20)),
                                 rope_shift=16, out_dtype=BF16)
            a = _gmlp(proj, gmlp_ws[j], gmlp_bs[j], gmlp_v_gain[j])
            b = _diff_attention(proj, diff_lambda[j].astype(F32), diff_sub_gain[j], lam_init)
        else:
            proj, stream = _proj(stream, peer_out, prev_mod, mod, norm_g[i, 0], w_in_odd[j].astype(BF16),
                                 cos_o, sin_o, rope_slabs=tuple(range(12, 16)) + tuple(range(28, 32)),
                                 rope_shift=32, out_dtype=F32)
            o_h, o_r = _scans(proj, lb_all[i], ret_log_decay[j])
            a, b = _odd_finish(o_h, o_r, proj, hgrn_gain[j], ret_gain[j])
        stream, h2, q = _out_proj(stream, a, b, w_out[i], mod, norm_g[i, 1], peer_wq[i])
        peer_out = _peer(h2.reshape(nb * t, d), q.reshape(nb * t, -1), peer_sub_keys[i],
                         peer_u[i], peer_v[i]).reshape(nb, t, d)
        prev_mod = mod
    return _final_norm(stream, peer_out, prev_mod, final_g, n_ctx)
```

```python
import functools
import math

import jax
import jax.numpy as jnp
from jax import lax
from jax.experimental import pallas as pl
from jax.experimental.pallas import tpu as pltpu

F32 = jnp.float32
BF16 = jnp.bfloat16

D_MODEL = 1024
MIX_W = D_MODEL // 2
HEAD_W = 128
N_HEADS = MIX_W // HEAD_W
GRID_W = 64
GMLP_CHUNK = 128
SCAN_CHUNK = 64
HGRN_SUB = 16
PEER_HEADS = 8
N_KEYS = 128
PK_TOPK = 16
ROPE_BASE = 10000.0
EPS = 1e-6
DIFF_SCALE = 64 ** -0.5
C_SCALE = HEAD_W ** -0.5
D_SCALE = HEAD_W ** -0.5

TOK_BLOCK = 256
ATTN_KV_TILE = 768
PEER_TOK = 512
PEER_EXP = 1024
PEER_ROWS = PEER_EXP // N_KEYS
PEER_JW = 32
VMEM_LIMIT = 56 * 1024 * 1024

NT_DIMS = (((1,), (1,)), ((), ()))
TN_DIMS = (((0,), (0,)), ((), ()))


def _cparams(sem):
    return pltpu.CompilerParams(dimension_semantics=sem, vmem_limit_bytes=VMEM_LIMIT)


def _gelu(x):
    return 0.5 * x * (1.0 + jnp.tanh(0.7978845608028654 * (x + 0.044715 * x * x * x)))


def _sigmoid(x):
    return 1.0 / (1.0 + jnp.exp(-x))


def _mod_row(which, n_batch):
    def index(b, t):
        return (jnp.where(t == 0, n_batch, b) * 6 + which, 0, 0)
    return index


def _ada_kernel(c_ref, w_ref, b_ref, o_ref):
    c = c_ref[...]
    a = (c * _sigmoid(c)).astype(BF16)
    o_ref[0] = jnp.dot(a, w_ref[0].astype(BF16), preferred_element_type=F32) + b_ref[0]


def _ada_mod(cvec, w_ada, b_ada):
    depth, d, n = w_ada.shape
    tn = 1536
    return pl.pallas_call(
        _ada_kernel,
        grid=(depth, n // tn),
        in_specs=[
            pl.BlockSpec((8, d), lambda i, j: (0, 0)),
            pl.BlockSpec((1, d, tn), lambda i, j: (i, 0, j)),
            pl.BlockSpec((1, 1, tn), lambda i, j: (i, 0, j)),
        ],
        out_specs=pl.BlockSpec((1, 8, tn), lambda i, j: (i, 0, j)),
        out_shape=jax.ShapeDtypeStruct((depth, 8, n), F32),
        compiler_params=_cparams(("parallel", "parallel")),
    )(cvec, w_ada, b_ada.reshape(depth, 1, n))


def _rope_slab(seg, cos, sin, shift):
    lane = lax.broadcasted_iota(jnp.int32, seg.shape, 1)
    first = (lane % (2 * shift)) < shift
    partner = jnp.where(first, pltpu.roll(seg, HEAD_W - shift, 1), pltpu.roll(seg, shift, 1))
    return seg * cos + partner * sin


def _proj_kernel(*refs, rope_slabs, rope_shift, add_peer):
    if add_peer:
        x_ref, p_ref, g2_ref, gn_ref, sh_ref, sc_ref, w_ref, cos_ref, sin_ref, o_ref, s_ref = refs
        x = x_ref[0] + g2_ref[0] * p_ref[0]
        s_ref[0] = x
    else:
        x_ref, gn_ref, sh_ref, sc_ref, w_ref, cos_ref, sin_ref, o_ref = refs
        x = x_ref[0]
    h = x * lax.rsqrt(jnp.mean(x * x, axis=-1, keepdims=True) + EPS) * gn_ref[...]
    h = h * (1.0 + sc_ref[0]) + sh_ref[0]
    acc = jnp.dot(h.astype(BF16), w_ref[...], preferred_element_type=F32)
    o_ref[0] = acc.astype(o_ref.dtype)
    cos = cos_ref[...]
    sin = sin_ref[...]
    for c in rope_slabs:
        seg = acc[:, c * HEAD_W:(c + 1) * HEAD_W]
        o_ref[0, :, c * HEAD_W:(c + 1) * HEAD_W] = _rope_slab(seg, cos, sin, rope_shift).astype(o_ref.dtype)


def _proj(x, peer, prev_mod, mod, gn, w, cos, sin, *, rope_slabs, rope_shift, out_dtype):
    nb, t, d = x.shape
    n = w.shape[1]
    blk = pl.BlockSpec((1, TOK_BLOCK, d), lambda b, i: (b, i, 0))
    vec = lambda which: pl.BlockSpec((1, 1, d), _mod_row(which, nb))
    in_specs = [blk]
    args = [x]
    if peer is not None:
        in_specs += [blk, vec(5)]
        args += [peer, prev_mod]
    in_specs += [
        pl.BlockSpec((1, d), lambda b, i: (0, 0)),
        vec(0), vec(1),
        pl.BlockSpec((d, n), lambda b, i: (0, 0)),
        pl.BlockSpec((TOK_BLOCK, HEAD_W), lambda b, i: (i, 0)),
        pl.BlockSpec((TOK_BLOCK, HEAD_W), lambda b, i: (i, 0)),
    ]
    args += [gn.reshape(1, d), mod, mod, w, cos, sin]
    out_specs = [pl.BlockSpec((1, TOK_BLOCK, n), lambda b, i: (b, i, 0))]
    out_shape = [jax.ShapeDtypeStruct((nb, t, n), out_dtype)]
    if peer is not None:
        out_specs.append(blk)
        out_shape.append(jax.ShapeDtypeStruct((nb, t, d), F32))
    res = pl.pallas_call(
        functools.partial(_proj_kernel, rope_slabs=rope_slabs, rope_shift=rope_shift,
                          add_peer=peer is not None),
        grid=(nb, t // TOK_BLOCK),
        in_specs=in_specs,
        out_specs=out_specs,
        out_shape=out_shape,
        compiler_params=_cparams(("parallel", "parallel")),
    )(*args)
    return (res[0], res[1]) if peer is not None else (res[0], x)


def _attn_kernel(lv_ref, q_ref, k_ref, v_ref, sg_ref, o_ref, *, n_kv, kv_tile, lam_init):
    qi = pl.program_id(2)
    q = q_ref[0]
    lane = lax.broadcasted_iota(jnp.int32, q.shape, 1)
    qs = q * jnp.asarray(DIFF_SCALE, q.dtype)
    zero = jnp.zeros_like(qs)
    q1 = jnp.where(lane < HEAD_W // 2, qs, zero)
    q2 = jnp.where(lane < HEAD_W // 2, zero, qs)
    tq = q.shape[0]
    qq = jnp.concatenate([q1, q2], axis=0)

    def online(rows, carry):
        m, l, a = carry
        k = k_ref[0, rows, :]
        v = v_ref[0, rows, :]
        s = lax.dot_general(qq, k, NT_DIMS, preferred_element_type=F32)
        m_new = jnp.maximum(m, jnp.max(s, axis=-1, keepdims=True))
        alpha = jnp.exp(m - m_new)
        p = jnp.exp(s - m_new)
        l_new = alpha * l + jnp.sum(p, axis=-1, keepdims=True)
        a_new = alpha * a + jnp.dot(p.astype(v.dtype), v, preferred_element_type=F32)
        return m_new, l_new, a_new

    def finish(carry):
        m, l, a = carry
        lv = lv_ref[...]
        lam = (jnp.exp(jnp.sum(lv[0:1] * lv[1:2], axis=-1, keepdims=True))
               - jnp.exp(jnp.sum(lv[2:3] * lv[3:4], axis=-1, keepdims=True)) + lam_init)
        o = a / l
        o = o[:tq] - lam * o[tq:]
        o = o * lax.rsqrt(jnp.mean(o * o, axis=-1, keepdims=True) + EPS) * sg_ref[...]
        o_ref[0] = (o * (1.0 - lam_init)).astype(o_ref.dtype)

    init = (jnp.full((2 * tq, 1), -jnp.inf, F32), jnp.zeros((2 * tq, 1), F32),
            jnp.zeros((2 * tq, HEAD_W), F32))

    @pl.when(qi == 0)
    def _():
        finish(online(pl.ds(0, TOK_BLOCK), init))

    @pl.when(qi > 0)
    def _():
        def step(c, carry):
            return online(pl.ds(pl.multiple_of(c * kv_tile, kv_tile), kv_tile), carry)
        finish(lax.fori_loop(0, n_kv, step, init))


def _diff_attention(proj, lam_vecs, sub_gain, lam_init):
    nb, t, _ = proj.shape
    n_q = t // TOK_BLOCK
    kv_tile = ATTN_KV_TILE if t % ATTN_KV_TILE == 0 else TOK_BLOCK
    return pl.pallas_call(
        functools.partial(_attn_kernel, n_kv=t // kv_tile, kv_tile=kv_tile, lam_init=lam_init),
        grid=(nb, N_HEADS, n_q),
        in_specs=[
            pl.BlockSpec((4, HEAD_W // 2), lambda b, h, i: (0, 0)),
            pl.BlockSpec((1, TOK_BLOCK, HEAD_W), lambda b, h, i: (b, i, 4 * N_HEADS + h)),
            pl.BlockSpec((1, t, HEAD_W), lambda b, h, i: (b, 0, h)),
            pl.BlockSpec((1, t, HEAD_W), lambda b, h, i: (b, 0, N_HEADS + h)),
            pl.BlockSpec((1, HEAD_W), lambda b, h, i: (0, 0)),
        ],
        out_specs=pl.BlockSpec((1, TOK_BLOCK, HEAD_W), lambda b, h, i: (b, i, h)),
        out_shape=jax.ShapeDtypeStruct((nb, t, MIX_W), BF16),
        compiler_params=_cparams(("parallel", "parallel", "arbitrary")),
    )(lam_vecs, proj, proj, proj, sub_gain.reshape(1, HEAD_W))


def _gmlp_kernel(u_ref, v_ref, ws_ref, bs_ref, vg_ref, o_ref):
    u = _gelu(u_ref[0].astype(F32))
    v = _gelu(v_ref[0].astype(F32))
    vc = v - jnp.mean(v, axis=-1, keepdims=True)
    vn = vc * lax.rsqrt(jnp.mean(vc * vc, axis=-1, keepdims=True) + EPS) * vg_ref[...]
    vn = vn.astype(BF16)
    for ch in range(TOK_BLOCK // GMLP_CHUNK):
        rows = slice(ch * GMLP_CHUNK, (ch + 1) * GMLP_CHUNK)
        for g in range(N_HEADS):
            cols = slice(g * HEAD_W, (g + 1) * HEAD_W)
            z = jnp.dot(ws_ref[g], vn[rows, cols], preferred_element_type=F32) + bs_ref[g]
            o_ref[0, rows, cols] = (u[rows, cols] * z).astype(o_ref.dtype)


def _gmlp(proj, ws, bs, v_gain):
    nb, t, _ = proj.shape
    bs_b = jnp.broadcast_to(bs[:, :, None], (N_HEADS, GMLP_CHUNK, HEAD_W)).astype(F32)
    return pl.pallas_call(
        _gmlp_kernel,
        grid=(nb, t // TOK_BLOCK),
        in_specs=[
            pl.BlockSpec((1, TOK_BLOCK, MIX_W), lambda b, i: (b, i, 2)),
            pl.BlockSpec((1, TOK_BLOCK, MIX_W), lambda b, i: (b, i, 3)),
            pl.BlockSpec((N_HEADS, GMLP_CHUNK, GMLP_CHUNK), lambda b, i: (0, 0, 0)),
            pl.BlockSpec((N_HEADS, GMLP_CHUNK, HEAD_W), lambda b, i: (0, 0, 0)),
            pl.BlockSpec((1, MIX_W), lambda b, i: (0, 0)),
        ],
        out_specs=pl.BlockSpec((1, TOK_BLOCK, MIX_W), lambda b, i: (b, i, 0)),
        out_shape=jax.ShapeDtypeStruct((nb, t, MIX_W), BF16),
        compiler_params=_cparams(("parallel", "parallel")),
    )(proj, proj, ws.astype(BF16), bs_b, v_gain.reshape(1, MIX_W))


def _scan_block_index(n_blk, reverse):
    if not reverse:
        return lambda i: i
    return lambda i: jnp.where(i == 0, 0, n_blk - i)


def _hgrn_chunk(q, k, v, logf, st_ref, reverse):
    c = q.shape[0]
    row = lax.broadcasted_iota(jnp.int32, (c, c), 0)
    col = lax.broadcasted_iota(jnp.int32, (c, c), 1)
    tri = ((col >= row) if reverse else (col <= row)).astype(F32)
    cum = jnp.dot(tri, logf, preferred_element_type=F32, precision=lax.Precision.HIGHEST)
    tot = jnp.sum(logf, axis=0, keepdims=True)
    st = st_ref[...]
    o_inter = lax.dot_general(q * jnp.exp(cum), st, NT_DIMS, preferred_element_type=F32)
    t8 = lax.broadcasted_iota(jnp.int32, (8, 1), 0)
    pieces = []
    for blk in range(c // HGRN_SUB):
        lo = blk * HGRN_SUB
        rows = slice(lo, lo + HGRN_SUB)
        o_blk = o_inter[rows]
        earlier = slice(lo + HGRN_SUB, c) if reverse else slice(0, lo)
        if earlier.stop > earlier.start:
            edge = lo + HGRN_SUB if reverse else lo - 1
            ref = cum[edge:edge + 1]
            qt = q[rows] * jnp.exp(cum[rows] - ref)
            kt = k[earlier] * jnp.exp(ref - cum[earlier])
            att = lax.dot_general(qt, kt, NT_DIMS, preferred_element_type=F32)
            o_blk = o_blk + jnp.dot(att, v[earlier], preferred_element_type=F32)
        halves = [o_blk[0:8], o_blk[8:16]]
        for sl in range(HGRN_SUB):
            s = lo + sl
            needed = [hh for hh in range(2) if ((8 * hh <= sl) if reverse else (8 * hh + 7 >= sl))]
            for hh in needed:
                r = slice(lo + 8 * hh, lo + 8 * hh + 8)
                w = q[r] * jnp.exp(jnp.minimum(cum[r] - cum[s:s + 1], 0.0)) * k[s:s + 1]
                a = jnp.sum(w, axis=-1, keepdims=True)
                t_loc = t8 + 8 * hh
                valid = (t_loc <= sl) if reverse else (t_loc >= sl)
                halves[hh] = halves[hh] + jnp.where(valid, a, 0.0) * v[s:s + 1]
        pieces += halves
    kd = k * jnp.exp(tot - cum)
    st_ref[...] = st * jnp.exp(tot) + lax.dot_general(v, kd, TN_DIMS, preferred_element_type=F32)
    return pieces


def _hgrn_kernel(fl_ref, i_ref, hq_ref, lb_ref, o_ref, st_ref, *, reverse):
    @pl.when(pl.program_id(2) == 0)
    def _():
        st_ref[...] = jnp.zeros_like(st_ref)

    lb = lb_ref[0]
    n_chunks = TOK_BLOCK // SCAN_CHUNK

    def chunk(j, carry):
        ci = n_chunks - 1 - j if reverse else j
        base = pl.multiple_of(ci * SCAN_CHUNK, SCAN_CHUNK)
        rows = pl.ds(base, SCAN_CHUNK)
        f = lb + (1.0 - lb) * _sigmoid(fl_ref[0, rows, :])
        hq = hq_ref[0, rows, :]
        q = hq * _sigmoid(hq) * C_SCALE
        pieces = _hgrn_chunk(q, 1.0 - f, i_ref[0, rows, :], jnp.log(f), st_ref, reverse)
        for p, piece in enumerate(pieces):
            o_ref[0, pl.ds(pl.multiple_of(base + 8 * p, 8), 8), :] = piece
        return carry

    lax.fori_loop(0, n_chunks, chunk, 0)


def _ret_kernel(rk_ref, rv_ref, rq_ref, dec_ref, o_ref, st_ref, *, reverse):
    @pl.when(pl.program_id(2) == 0)
    def _():
        st_ref[...] = jnp.zeros_like(st_ref)

    c = TOK_BLOCK
    lg = -jnp.exp(dec_ref[0])
    row = lax.broadcasted_iota(jnp.int32, (c, c), 0)
    col = lax.broadcasted_iota(jnp.int32, (c, c), 1)
    lag = (col - row) if reverse else (row - col)
    decay = jnp.where(lag >= 0, jnp.exp(lag.astype(F32) * lg[:, 0:1]), 0.0)
    t_idx = lax.broadcasted_iota(jnp.int32, (c, 1), 0)
    steps = (c - t_idx) if reverse else (t_idx + 1)
    cum = steps.astype(F32) * lg
    tot = c * lg
    q = rq_ref[0]
    k = rk_ref[0] * D_SCALE
    v = rv_ref[0]
    st = st_ref[...]
    o = lax.dot_general(q * jnp.exp(cum), st, NT_DIMS, preferred_element_type=F32)
    att = lax.dot_general(q, k, NT_DIMS, preferred_element_type=F32) * decay
    o_ref[0] = o + jnp.dot(att, v, preferred_element_type=F32)
    kd = k * jnp.exp(tot - cum)
    st_ref[...] = st * jnp.exp(tot) + lax.dot_general(v, kd, TN_DIMS, preferred_element_type=F32)


def _scans(proj, lb, ret_decay):
    nb, t, _ = proj.shape
    n_blk = t // TOK_BLOCK
    grid = (nb, N_HEADS, n_blk)
    sem = ("parallel", "parallel", "arbitrary")
    state = pltpu.VMEM((HEAD_W, HEAD_W), F32)
    out_shape = jax.ShapeDtypeStruct((nb, t, MIX_W), F32)
    dec = jnp.broadcast_to(ret_decay.reshape(2 * N_HEADS, 1, 1), (2 * N_HEADS, 1, HEAD_W)).astype(F32)
    outs = []
    for kind in ("hgrn", "ret"):
        for d in (0, 1):
            blk = _scan_block_index(n_blk, d == 1)
            col = lambda base, blk=blk: pl.BlockSpec(
                (1, TOK_BLOCK, HEAD_W), lambda b, h, i: (b, blk(i), base + h))
            out_spec = pl.BlockSpec((1, TOK_BLOCK, HEAD_W), lambda b, h, i, blk=blk: (b, blk(i), h))
            if kind == "hgrn":
                body = functools.partial(_hgrn_kernel, reverse=d == 1)
                in_specs = [col(4 * d), col(8), col(20),
                            pl.BlockSpec((1, 1, HEAD_W), lambda b, h, i: (h, 0, 0))]
                args = (proj, proj, proj, lb.reshape(N_HEADS, 1, HEAD_W))
            else:
                body = functools.partial(_ret_kernel, reverse=d == 1)
                in_specs = [col(12), col(16), col(28),
                            pl.BlockSpec((1, 1, HEAD_W), lambda b, h, i, d=d: (d * N_HEADS + h, 0, 0))]
                args = (proj, proj, proj, dec)
            outs.append(pl.pallas_call(
                body, grid=grid, in_specs=in_specs, out_specs=out_spec, out_shape=out_shape,
                scratch_shapes=[state], compiler_params=_cparams(sem),
            )(*args))
    return outs


def _odd_finish_kernel(ohf_ref, ohb_ref, orf_ref, orb_ref, hg_ref, rg_ref, hgain_ref, rgain_ref,
                       a_ref, b_ref):
    oh = ohf_ref[0] + ohb_ref[0]
    orr = orf_ref[0] + orb_ref[0]
    gh = oh * _sigmoid(hg_ref[0])
    rg = rg_ref[0]
    for h in range(N_HEADS):
        cols = slice(h * HEAD_W, (h + 1) * HEAD_W)
        x = gh[:, cols]
        a_ref[0, :, cols] = (x * lax.rsqrt(jnp.mean(x * x, axis=-1, keepdims=True) + EPS)
                             * hgain_ref[...]).astype(a_ref.dtype)
        y = orr[:, cols]
        yc = y - jnp.mean(y, axis=-1, keepdims=True)
        yn = yc * lax.rsqrt(jnp.mean(yc * yc, axis=-1, keepdims=True) + EPS) * rgain_ref[...]
        r = rg[:, cols]
        b_ref[0, :, cols] = (yn * (r * _sigmoid(r))).astype(b_ref.dtype)


def _odd_finish(scan_outs, proj, hg_gain, ret_gain):
    nb, t, _ = scan_outs[0].shape
    o_spec = pl.BlockSpec((1, TOK_BLOCK, MIX_W), lambda b, i: (b, i, 0))
    return pl.pallas_call(
        _odd_finish_kernel,
        grid=(nb, t // TOK_BLOCK),
        in_specs=[o_spec, o_spec, o_spec, o_spec,
                  pl.BlockSpec((1, TOK_BLOCK, MIX_W), lambda b, i: (b, i, 6)),
                  pl.BlockSpec((1, TOK_BLOCK, MIX_W), lambda b, i: (b, i, 8)),
                  pl.BlockSpec((1, HEAD_W), lambda b, i: (0, 0)),
                  pl.BlockSpec((1, HEAD_W), lambda b, i: (0, 0))],
        out_specs=[o_spec, o_spec],
        out_shape=[jax.ShapeDtypeStruct((nb, t, MIX_W), BF16)] * 2,
        compiler_params=_cparams(("parallel", "parallel")),
    )(*scan_outs, proj, proj, hg_gain.reshape(1, HEAD_W), ret_gain.reshape(1, HEAD_W))


def _out_kernel(x_ref, a_ref, b_ref, wa_ref, wb_ref, g1_ref, gn_ref, sh_ref, sc_ref, wq_ref,
                xo_ref, h_ref, q_ref):
    y = (jnp.dot(a_ref[0], wa_ref[...], preferred_element_type=F32)
         + jnp.dot(b_ref[0], wb_ref[...], preferred_element_type=F32))
    x = x_ref[0] + g1_ref[0] * y
    xo_ref[0] = x
    h = x * lax.rsqrt(jnp.mean(x * x, axis=-1, keepdims=True) + EPS) * gn_ref[...]
    h = (h * (1.0 + sc_ref[0]) + sh_ref[0]).astype(BF16)
    h_ref[0] = h
    q_ref[0] = jnp.dot(h, wq_ref[...], preferred_element_type=F32).astype(q_ref.dtype)


def _out_proj(x, a, b, w_out, mod, gn, wq):
    nb, t, d = x.shape
    nq = wq.shape[1]
    blk = lambda w: pl.BlockSpec((1, TOK_BLOCK, w), lambda bb, i: (bb, i, 0))
    vec = lambda which: pl.BlockSpec((1, 1, d), _mod_row(which, nb))
    full = lambda r, c: pl.BlockSpec((r, c), lambda bb, i: (0, 0))
    w16 = w_out.astype(BF16)
    return pl.pallas_call(
        _out_kernel,
        grid=(nb, t // TOK_BLOCK),
        in_specs=[blk(d), blk(MIX_W), blk(MIX_W), full(MIX_W, d), full(MIX_W, d),
                  vec(2), full(1, d), vec(3), vec(4), full(d, nq)],
        out_specs=[blk(d), blk(d), blk(nq)],
        out_shape=[jax.ShapeDtypeStruct((nb, t, d), F32),
                   jax.ShapeDtypeStruct((nb, t, d), BF16),
                   jax.ShapeDtypeStruct((nb, t, nq), BF16)],
        compiler_params=_cparams(("parallel", "parallel")),
    )(x, a, b, w16[:MIX_W], w16[MIX_W:], mod, gn.reshape(1, d), mod, mod, wq.astype(BF16))


def _top_values(work, n):
    rows = []
    for _ in range(n):
        m = jnp.max(work, axis=0, keepdims=True)
        rows.append(m)
        work = jnp.where(work == m, -jnp.inf, work)
    return rows


def _peer_scores(q_ref, keys_ref, s1_ref, e1_ref, th_ref, e0_ref, top_ref):
    n_top = PK_TOPK + 1
    top_ref[...] = jnp.full(top_ref.shape, -jnp.inf, F32)
    for h in range(PEER_HEADS):
        scores, tops = [], []
        for side in range(2):
            slab = (2 * h + side) * N_KEYS
            s = lax.dot_general(keys_ref[h, side], q_ref[:, slab:slab + N_KEYS], NT_DIMS,
                                preferred_element_type=F32)
            vals = _top_values(s, n_top)
            for r in range(n_top):
                top_ref[side, r:r + 1, :] = vals[r]
            scores.append(s)
            tops.append(vals)
        t0, t1 = tops
        cands = [t0[0] + top_ref[1]]
        cands += [t0[a] + top_ref[1, 0:8, :] for a in range(1, 8)]
        cands += [top_ref[0, 8:, :] + t1[0]]
        best = _top_values(jnp.concatenate(cands, axis=0), n_top)
        z = jnp.zeros_like(best[0])
        for r in range(PK_TOPK):
            z = z + jnp.exp(best[r] - best[0])
        tau = 0.5 * (best[PK_TOPK - 1] + best[PK_TOPK])
        th_ref[h] = tau - scores[0]
        e0_ref[h] = jnp.exp(scores[0] - t0[0])
        s1_ref[h] = scores[1]
        e1_ref[h] = jnp.exp(scores[1] - t1[0]) / z


def _peer_gate_piece(piece, s1_ref, e1_ref, rows_ref, act_ref, w_ref):
    ct, jq = divmod(piece, N_KEYS // PEER_JW)
    lanes = slice(ct * N_KEYS, (ct + 1) * N_KEYS)
    jrows = slice(jq * PEER_JW, (jq + 1) * PEER_JW)
    accs = [jnp.zeros((PEER_JW, N_KEYS), F32) for _ in range(PEER_ROWS)]
    for h in range(PEER_HEADS):
        th = rows_ref[0, h, :, lanes]
        e0 = rows_ref[1, h, :, lanes]
        s1 = s1_ref[h, jrows, lanes]
        e1 = e1_ref[h, jrows, lanes]
        for il in range(PEER_ROWS):
            accs[il] = accs[il] + jnp.where(s1 >= th[il:il + 1], e1 * e0[il:il + 1], 0.0)
    for il in range(PEER_ROWS):
        rows = slice(il * N_KEYS + jq * PEER_JW, il * N_KEYS + (jq + 1) * PEER_JW)
        w_ref[rows, lanes] = (accs[il] * _gelu(act_ref[rows, lanes])).astype(w_ref.dtype)


def _peer_kernel(h_ref, q_ref, keys_ref, u_first_ref, u_next_ref, vt_ref, vt_last_ref, o_ref,
                 s1_ref, e1_ref, th_ref, e0_ref, top_ref, rows_ref, act_a, act_b, w_a, w_b, acc_ref):
    ec = pl.program_id(1)
    n_chunks = pl.num_programs(1)

    @pl.when(ec == 0)
    def _():
        _peer_scores(q_ref, keys_ref, s1_ref, e1_ref, th_ref, e0_ref, top_ref)
        act_a[...] = lax.dot_general(u_first_ref[...], h_ref[...], NT_DIMS, preferred_element_type=F32)
        w_b[...] = jnp.zeros_like(w_b)
        acc_ref[...] = jnp.zeros_like(acc_ref)

    first_key = pl.multiple_of(ec * PEER_ROWS, PEER_ROWS)
    rows_ref[0] = th_ref[:, pl.ds(first_key, PEER_ROWS), :]
    rows_ref[1] = e0_ref[:, pl.ds(first_key, PEER_ROWS), :]

    n_gate = (PEER_TOK // N_KEYS) * (N_KEYS // PEER_JW)
    n_split = 4

    def matmul_piece(k, act_next, w_prev):
        if k < n_split:
            r = slice(k * (acc_ref.shape[0] // n_split), (k + 1) * (acc_ref.shape[0] // n_split))
            acc_ref[r, :] += jnp.dot(vt_ref[r, :], w_prev[...], preferred_element_type=F32)
        else:
            k -= n_split
            r = slice(k * (PEER_EXP // n_split), (k + 1) * (PEER_EXP // n_split))
            act_next[r, :] = lax.dot_general(u_next_ref[r, :], h_ref[...], NT_DIMS,
                                             preferred_element_type=F32)

    def step(act_cur, act_next, w_cur, w_prev):
        for piece in range(n_gate):
            if piece % (n_gate // (2 * n_split)) == 0:
                matmul_piece(piece // (n_gate // (2 * n_split)), act_next, w_prev)
            _peer_gate_piece(piece, s1_ref, e1_ref, rows_ref, act_cur, w_cur)

    @pl.when(ec % 2 == 0)
    def _():
        step(act_a, act_b, w_a, w_b)

    @pl.when(ec % 2 == 1)
    def _():
        step(act_b, act_a, w_b, w_a)

    @pl.when(ec == n_chunks - 1)
    def _():
        w_last = w_b
        acc = acc_ref[...] + jnp.dot(vt_last_ref[...], w_last[...], preferred_element_type=F32)
        o_ref[...] = acc.T


def _peer(h2, q, sub_keys, u_tab, v_tab):
    n, d = h2.shape
    n_exp = u_tab.shape[0]
    n_chunks = n_exp // PEER_EXP
    assert n_chunks % 2 == 0
    nq = q.shape[1]
    heads_tile = pltpu.VMEM((PEER_HEADS, N_KEYS, PEER_TOK), F32)
    act_tile = pltpu.VMEM((PEER_EXP, PEER_TOK), F32)
    w_tile = pltpu.VMEM((PEER_EXP, PEER_TOK), BF16)
    u16 = u_tab.astype(BF16)
    vt = v_tab.astype(BF16).T
    return pl.pallas_call(
        _peer_kernel,
        grid=(n // PEER_TOK, n_chunks),
        in_specs=[
            pl.BlockSpec((PEER_TOK, d), lambda t, e: (t, 0)),
            pl.BlockSpec((PEER_TOK, nq), lambda t, e: (t, 0)),
            pl.BlockSpec((PEER_HEADS, 2, N_KEYS, N_KEYS), lambda t, e: (0, 0, 0, 0)),
            pl.BlockSpec((PEER_EXP, d), lambda t, e: (0, 0)),
            pl.BlockSpec((PEER_EXP, d), lambda t, e: (jnp.minimum(e + 1, n_chunks - 1), 0)),
            pl.BlockSpec((d, PEER_EXP), lambda t, e: (0, jnp.maximum(e - 1, 0))),
            pl.BlockSpec((d, PEER_EXP), lambda t, e: (0, n_chunks - 1)),
        ],
        out_specs=pl.BlockSpec((PEER_TOK, d), lambda t, e: (t, 0)),
        out_shape=jax.ShapeDtypeStruct((n, d), F32),
        scratch_shapes=[heads_tile, heads_tile, heads_tile, heads_tile,
                        pltpu.VMEM((2, 24, PEER_TOK), F32),
                        pltpu.VMEM((2, PEER_HEADS, PEER_ROWS, PEER_TOK), F32),
                        act_tile, act_tile, w_tile, w_tile,
                        pltpu.VMEM((d, PEER_TOK), F32)],
        compiler_params=_cparams(("parallel", "arbitrary")),
    )(h2, q, sub_keys.astype(BF16), u16, u16, vt, vt)


def _final_kernel(x_ref, p_ref, g2_ref, g_ref, o_ref):
    x = x_ref[0] + g2_ref[0] * p_ref[0]
    o_ref[0] = x * lax.rsqrt(jnp.mean(x * x, axis=-1, keepdims=True) + EPS) * g_ref[...]


def _final_norm(x, peer, mod, final_g, n_ctx):
    nb, t, d = x.shape
    skip = n_ctx // TOK_BLOCK
    blk = pl.BlockSpec((1, TOK_BLOCK, d), lambda b, i: (b, i + skip, 0))
    return pl.pallas_call(
        _final_kernel,
        grid=(nb, (t - n_ctx) // TOK_BLOCK),
        in_specs=[blk, blk,
                  pl.BlockSpec((1, 1, d), lambda b, i: (b * 6 + 5, 0, 0)),
                  pl.BlockSpec((1, d), lambda b, i: (0, 0))],
        out_specs=pl.BlockSpec((1, TOK_BLOCK, d), lambda b, i: (b, i, 0)),
        out_shape=jax.ShapeDtypeStruct((nb, t - n_ctx, d), F32),
        compiler_params=_cparams(("parallel", "parallel")),
    )(x, peer, mod, final_g.reshape(1, d))


def _rope_tables(n_ctx, n_tok, half):
    lane = jnp.arange(HEAD_W)
    part_w = 2 * half
    inv = ROPE_BASE ** (-(lane % half).astype(F32) / half)
    use_col = (lane // part_w) % 2 == 1
    tok = jnp.arange(n_tok)
    pos = jnp.where(use_col[None, :], (tok % GRID_W)[:, None], (tok // GRID_W)[:, None]).astype(F32)
    ang = pos * inv[None, :]
    sign = jnp.where((lane % part_w) < half, -1.0, 1.0)
    cos = jnp.concatenate([jnp.ones((n_ctx, HEAD_W), F32), jnp.cos(ang)], axis=0)
    sin = jnp.concatenate([jnp.zeros((n_ctx, HEAD_W), F32), jnp.sin(ang) * sign[None, :]], axis=0)
    return cos, sin


def kernel(x, c, ctx, c_ctx, w_ada, b_ada, norm_g, w_in_even, gmlp_ws, gmlp_bs, gmlp_v_gain, diff_lambda, diff_sub_gain, w_in_odd, hgrn_lower_bounds, hgrn_gain, ret_log_decay, ret_gain, w_out, peer_wq, peer_sub_keys, peer_u, peer_v, final_g):
    nb, n_tok, d = x.shape
    n_ctx = ctx.shape[1]
    depth = w_ada.shape[0]
    assert d == D_MODEL and n_ctx == TOK_BLOCK and n_tok % TOK_BLOCK == 0 and nb < 8
    assert (nb * (n_ctx + n_tok)) % PEER_TOK == 0
    t = n_ctx + n_tok

    stream = jnp.concatenate([ctx, x], axis=1)
    cvec = jnp.zeros((8, d), F32).at[:nb].set(c).at[nb].set(c_ctx)
    mod_all = _ada_mod(cvec, w_ada, b_ada).reshape(depth, 8 * 6, 1, d)

    lb_all = jnp.cumsum(jax.nn.softmax(hgrn_lower_bounds.astype(F32), axis=0), axis=0)
    lb_all = lb_all - lb_all[0]
    cos_e, sin_e = _rope_tables(n_ctx, n_tok, 16)
    cos_o, sin_o = _rope_tables(n_ctx, n_tok, 32)

    peer_out = None
    prev_mod = None
    for i in range(depth):
        j = i // 2
        mod = mod_all[i]
        if i % 2 == 0:
            lam_init = 0.8 - 0.6 * math.exp(-0.3 * i)
            proj, stream = _proj(stream, peer_out, prev_mod, mod, norm_g[i, 0], w_in_even[j].astype(BF16),
                                 cos_e, sin_e, rope_slabs=tuple(range(0, 4)) + tuple(range(16, 20)),
                                 rope_shift=16, out_dtype=BF16)
            a = _gmlp(proj, gmlp_ws[j], gmlp_bs[j], gmlp_v_gain[j])
            b = _diff_attention(proj, diff_lambda[j].astype(F32), diff_sub_gain[j], lam_init)
        else:
            proj, stream = _proj(stream, peer_out, prev_mod, mod, norm_g[i, 0], w_in_odd[j].astype(BF16),
                                 cos_o, sin_o, rope_slabs=tuple(range(12, 16)) + tuple(range(28, 32)),
                                 rope_shift=32, out_dtype=F32)
            a, b = _odd_finish(_scans(proj, lb_all[i], ret_log_decay[j]), proj, hgrn_gain[j], ret_gain[j])
        stream, h2, q = _out_proj(stream, a, b, w_out[i], mod, norm_g[i, 1], peer_wq[i])
        peer_out = _peer(h2.reshape(nb * t, d), q.reshape(nb * t, -1), peer_sub_keys[i],
                         peer_u[i], peer_v[i]).reshape(nb, t, d)
        prev_mod = mod
    return _final_norm(stream, peer_out, prev_mod, final_g, n_ctx)
```

```python
import functools
import math

import jax
import jax.numpy as jnp
from jax import lax
from jax.experimental import pallas as pl
from jax.experimental.pallas import tpu as pltpu

F32 = jnp.float32
BF16 = jnp.bfloat16

D_MODEL = 1024
MIX_W = D_MODEL // 2
HEAD_W = 128
N_HEADS = MIX_W // HEAD_W
GRID_W = 64
GMLP_CHUNK = 128
SCAN_CHUNK = 64
HGRN_SUB = 16
HGRN_GROUP = 2
PEER_HEADS = 8
N_KEYS = 128
PK_TOPK = 16
ROPE_BASE = 10000.0
EPS = 1e-6
DIFF_SCALE = 64 ** -0.5
C_SCALE = HEAD_W ** -0.5
D_SCALE = HEAD_W ** -0.5

TOK_BLOCK = 256
ATTN_KV_TILE = 768
PEER_TOK = 512
PEER_EXP = 1024
PEER_ROWS = PEER_EXP // N_KEYS
PEER_JW = 32
VMEM_LIMIT = 56 * 1024 * 1024

NT_DIMS = (((1,), (1,)), ((), ()))
TN_DIMS = (((0,), (0,)), ((), ()))


def _cparams(sem):
    return pltpu.CompilerParams(dimension_semantics=sem, vmem_limit_bytes=VMEM_LIMIT)


def _gelu(x):
    return 0.5 * x * (1.0 + jnp.tanh(0.7978845608028654 * (x + 0.044715 * x * x * x)))


def _sigmoid(x):
    return 1.0 / (1.0 + jnp.exp(-x))


def _mod_row(which, n_batch):
    def index(b, t):
        return (jnp.where(t == 0, n_batch, b) * 6 + which, 0, 0)
    return index


def _ada_kernel(c_ref, w_ref, b_ref, o_ref):
    c = c_ref[...]
    a = (c * _sigmoid(c)).astype(BF16)
    o_ref[0] = jnp.dot(a, w_ref[0].astype(BF16), preferred_element_type=F32) + b_ref[0]


def _ada_mod(cvec, w_ada, b_ada):
    depth, d, n = w_ada.shape
    tn = 1536
    return pl.pallas_call(
        _ada_kernel,
        grid=(depth, n // tn),
        in_specs=[
            pl.BlockSpec((8, d), lambda i, j: (0, 0)),
            pl.BlockSpec((1, d, tn), lambda i, j: (i, 0, j)),
            pl.BlockSpec((1, 1, tn), lambda i, j: (i, 0, j)),
        ],
        out_specs=pl.BlockSpec((1, 8, tn), lambda i, j: (i, 0, j)),
        out_shape=jax.ShapeDtypeStruct((depth, 8, n), F32),
        compiler_params=_cparams(("parallel", "parallel")),
    )(cvec, w_ada, b_ada.reshape(depth, 1, n))


def _rope_slab(seg, cos, sin, shift):
    lane = lax.broadcasted_iota(jnp.int32, seg.shape, 1)
    first = (lane % (2 * shift)) < shift
    partner = jnp.where(first, pltpu.roll(seg, HEAD_W - shift, 1), pltpu.roll(seg, shift, 1))
    return seg * cos + partner * sin


def _proj_kernel(*refs, rope_slabs, rope_shift, add_peer):
    if add_peer:
        x_ref, p_ref, g2_ref, gn_ref, sh_ref, sc_ref, w_ref, cos_ref, sin_ref, o_ref, s_ref = refs
        x = x_ref[0] + g2_ref[0] * p_ref[0]
        s_ref[0] = x
    else:
        x_ref, gn_ref, sh_ref, sc_ref, w_ref, cos_ref, sin_ref, o_ref = refs
        x = x_ref[0]
    h = x * lax.rsqrt(jnp.mean(x * x, axis=-1, keepdims=True) + EPS) * gn_ref[...]
    h = h * (1.0 + sc_ref[0]) + sh_ref[0]
    acc = jnp.dot(h.astype(BF16), w_ref[...], preferred_element_type=F32)
    o_ref[0] = acc.astype(o_ref.dtype)
    cos = cos_ref[...]
    sin = sin_ref[...]
    for c in rope_slabs:
        seg = acc[:, c * HEAD_W:(c + 1) * HEAD_W]
        o_ref[0, :, c * HEAD_W:(c + 1) * HEAD_W] = _rope_slab(seg, cos, sin, rope_shift).astype(o_ref.dtype)


def _proj(x, peer, prev_mod, mod, gn, w, cos, sin, *, rope_slabs, rope_shift, out_dtype):
    nb, t, d = x.shape
    n = w.shape[1]
    blk = pl.BlockSpec((1, TOK_BLOCK, d), lambda b, i: (b, i, 0))
    vec = lambda which: pl.BlockSpec((1, 1, d), _mod_row(which, nb))
    in_specs = [blk]
    args = [x]
    if peer is not None:
        in_specs += [blk, vec(5)]
        args += [peer, prev_mod]
    in_specs += [
        pl.BlockSpec((1, d), lambda b, i: (0, 0)),
        vec(0), vec(1),
        pl.BlockSpec((d, n), lambda b, i: (0, 0)),
        pl.BlockSpec((TOK_BLOCK, HEAD_W), lambda b, i: (i, 0)),
        pl.BlockSpec((TOK_BLOCK, HEAD_W), lambda b, i: (i, 0)),
    ]
    args += [gn.reshape(1, d), mod, mod, w, cos, sin]
    out_specs = [pl.BlockSpec((1, TOK_BLOCK, n), lambda b, i: (b, i, 0))]
    out_shape = [jax.ShapeDtypeStruct((nb, t, n), out_dtype)]
    if peer is not None:
        out_specs.append(blk)
        out_shape.append(jax.ShapeDtypeStruct((nb, t, d), F32))
    res = pl.pallas_call(
        functools.partial(_proj_kernel, rope_slabs=rope_slabs, rope_shift=rope_shift,
                          add_peer=peer is not None),
        grid=(nb, t // TOK_BLOCK),
        in_specs=in_specs,
        out_specs=out_specs,
        out_shape=out_shape,
        compiler_params=_cparams(("parallel", "parallel")),
    )(*args)
    return (res[0], res[1]) if peer is not None else (res[0], x)


def _attn_kernel(lv_ref, q_ref, k_ref, v_ref, sg_ref, o_ref, *, n_kv, kv_tile, lam_init):
    qi = pl.program_id(2)
    q = q_ref[0]
    lane = lax.broadcasted_iota(jnp.int32, q.shape, 1)
    qs = q * jnp.asarray(DIFF_SCALE, q.dtype)
    zero = jnp.zeros_like(qs)
    q1 = jnp.where(lane < HEAD_W // 2, qs, zero)
    q2 = jnp.where(lane < HEAD_W // 2, zero, qs)
    tq = q.shape[0]
    qq = jnp.concatenate([q1, q2], axis=0)

    def online(rows, carry):
        m, l, a = carry
        k = k_ref[0, rows, :]
        v = v_ref[0, rows, :]
        s = lax.dot_general(qq, k, NT_DIMS, preferred_element_type=F32)
        m_new = jnp.maximum(m, jnp.max(s, axis=-1, keepdims=True))
        alpha = jnp.exp(m - m_new)
        p = jnp.exp(s - m_new)
        l_new = alpha * l + jnp.sum(p, axis=-1, keepdims=True)
        a_new = alpha * a + jnp.dot(p.astype(v.dtype), v, preferred_element_type=F32)
        return m_new, l_new, a_new

    def finish(carry):
        m, l, a = carry
        lv = lv_ref[...]
        lam = (jnp.exp(jnp.sum(lv[0:1] * lv[1:2], axis=-1, keepdims=True))
               - jnp.exp(jnp.sum(lv[2:3] * lv[3:4], axis=-1, keepdims=True)) + lam_init)
        o = a / l
        o = o[:tq] - lam * o[tq:]
        o = o * lax.rsqrt(jnp.mean(o * o, axis=-1, keepdims=True) + EPS) * sg_ref[...]
        o_ref[0] = (o * (1.0 - lam_init)).astype(o_ref.dtype)

    init = (jnp.full((2 * tq, 1), -jnp.inf, F32), jnp.zeros((2 * tq, 1), F32),
            jnp.zeros((2 * tq, HEAD_W), F32))

    @pl.when(qi == 0)
    def _():
        finish(online(pl.ds(0, TOK_BLOCK), init))

    @pl.when(qi > 0)
    def _():
        def step(c, carry):
            return online(pl.ds(pl.multiple_of(c * kv_tile, kv_tile), kv_tile), carry)
        finish(lax.fori_loop(0, n_kv, step, init))


def _diff_attention(proj, lam_vecs, sub_gain, lam_init):
    nb, t, _ = proj.shape
    n_q = t // TOK_BLOCK
    kv_tile = ATTN_KV_TILE if t % ATTN_KV_TILE == 0 else TOK_BLOCK
    return pl.pallas_call(
        functools.partial(_attn_kernel, n_kv=t // kv_tile, kv_tile=kv_tile, lam_init=lam_init),
        grid=(nb, N_HEADS, n_q),
        in_specs=[
            pl.BlockSpec((4, HEAD_W // 2), lambda b, h, i: (0, 0)),
            pl.BlockSpec((1, TOK_BLOCK, HEAD_W), lambda b, h, i: (b, i, 4 * N_HEADS + h)),
            pl.BlockSpec((1, t, HEAD_W), lambda b, h, i: (b, 0, h)),
            pl.BlockSpec((1, t, HEAD_W), lambda b, h, i: (b, 0, N_HEADS + h)),
            pl.BlockSpec((1, HEAD_W), lambda b, h, i: (0, 0)),
        ],
        out_specs=pl.BlockSpec((1, TOK_BLOCK, HEAD_W), lambda b, h, i: (b, i, h)),
        out_shape=jax.ShapeDtypeStruct((nb, t, MIX_W), BF16),
        compiler_params=_cparams(("parallel", "parallel", "arbitrary")),
    )(lam_vecs, proj, proj, proj, sub_gain.reshape(1, HEAD_W))


def _gmlp_kernel(u_ref, v_ref, ws_ref, bs_ref, vg_ref, o_ref):
    u = _gelu(u_ref[0].astype(F32))
    v = _gelu(v_ref[0].astype(F32))
    vc = v - jnp.mean(v, axis=-1, keepdims=True)
    vn = vc * lax.rsqrt(jnp.mean(vc * vc, axis=-1, keepdims=True) + EPS) * vg_ref[...]
    vn = vn.astype(BF16)
    for ch in range(TOK_BLOCK // GMLP_CHUNK):
        rows = slice(ch * GMLP_CHUNK, (ch + 1) * GMLP_CHUNK)
        for g in range(N_HEADS):
            cols = slice(g * HEAD_W, (g + 1) * HEAD_W)
            z = jnp.dot(ws_ref[g], vn[rows, cols], preferred_element_type=F32) + bs_ref[g]
            o_ref[0, rows, cols] = (u[rows, cols] * z).astype(o_ref.dtype)


def _gmlp(proj, ws, bs, v_gain):
    nb, t, _ = proj.shape
    bs_b = jnp.broadcast_to(bs[:, :, None], (N_HEADS, GMLP_CHUNK, HEAD_W)).astype(F32)
    return pl.pallas_call(
        _gmlp_kernel,
        grid=(nb, t // TOK_BLOCK),
        in_specs=[
            pl.BlockSpec((1, TOK_BLOCK, MIX_W), lambda b, i: (b, i, 2)),
            pl.BlockSpec((1, TOK_BLOCK, MIX_W), lambda b, i: (b, i, 3)),
            pl.BlockSpec((N_HEADS, GMLP_CHUNK, GMLP_CHUNK), lambda b, i: (0, 0, 0)),
            pl.BlockSpec((N_HEADS, GMLP_CHUNK, HEAD_W), lambda b, i: (0, 0, 0)),
            pl.BlockSpec((1, MIX_W), lambda b, i: (0, 0)),
        ],
        out_specs=pl.BlockSpec((1, TOK_BLOCK, MIX_W), lambda b, i: (b, i, 0)),
        out_shape=jax.ShapeDtypeStruct((nb, t, MIX_W), BF16),
        compiler_params=_cparams(("parallel", "parallel")),
    )(proj, proj, ws.astype(BF16), bs_b, v_gain.reshape(1, MIX_W))


def _scan_block_index(n_blk, reverse):
    if not reverse:
        return lambda i: i
    return lambda i: jnp.where(i == 0, 0, n_blk - i)


def _hgrn_chunk(q, k, v, logf, st_ref, reverse):
    c = q.shape[0]
    row = lax.broadcasted_iota(jnp.int32, (c, c), 0)
    col = lax.broadcasted_iota(jnp.int32, (c, c), 1)
    tri = ((col >= row) if reverse else (col <= row)).astype(F32)
    cum = jnp.dot(tri, logf, preferred_element_type=F32, precision=lax.Precision.HIGHEST)
    tot = jnp.sum(logf, axis=0, keepdims=True)
    st = st_ref[...]
    o_inter = lax.dot_general(q * jnp.exp(cum), st, NT_DIMS, preferred_element_type=F32)
    t8 = lax.broadcasted_iota(jnp.int32, (8, 1), 0)
    pieces = []
    for blk in range(c // HGRN_SUB):
        lo = blk * HGRN_SUB
        rows = slice(lo, lo + HGRN_SUB)
        o_blk = o_inter[rows]
        earlier = slice(lo + HGRN_SUB, c) if reverse else slice(0, lo)
        if earlier.stop > earlier.start:
            edge = lo + HGRN_SUB if reverse else lo - 1
            ref = cum[edge:edge + 1]
            qt = q[rows] * jnp.exp(cum[rows] - ref)
            kt = k[earlier] * jnp.exp(ref - cum[earlier])
            att = lax.dot_general(qt, kt, NT_DIMS, preferred_element_type=F32)
            o_blk = o_blk + jnp.dot(att, v[earlier], preferred_element_type=F32)
        halves = [o_blk[0:8], o_blk[8:16]]
        for sl in range(HGRN_SUB):
            s = lo + sl
            needed = [hh for hh in range(2) if ((8 * hh <= sl) if reverse else (8 * hh + 7 >= sl))]
            for hh in needed:
                r = slice(lo + 8 * hh, lo + 8 * hh + 8)
                w = q[r] * jnp.exp(jnp.minimum(cum[r] - cum[s:s + 1], 0.0)) * k[s:s + 1]
                a = jnp.sum(w, axis=-1, keepdims=True)
                t_loc = t8 + 8 * hh
                valid = (t_loc <= sl) if reverse else (t_loc >= sl)
                halves[hh] = halves[hh] + jnp.where(valid, a, 0.0) * v[s:s + 1]
        pieces += halves
    kd = k * jnp.exp(tot - cum)
    st_ref[...] = st * jnp.exp(tot) + lax.dot_general(v, kd, TN_DIMS, preferred_element_type=F32)
    return pieces


def _hgrn_kernel(fl_ref, i_ref, hq_ref, lb_ref, o_ref, st_ref, *, reverse):
    @pl.when(pl.program_id(2) == 0)
    def _():
        st_ref[...] = jnp.zeros_like(st_ref)

    n_chunks = TOK_BLOCK // SCAN_CHUNK

    def chunk(j, carry):
        ci = n_chunks - 1 - j if reverse else j
        base = pl.multiple_of(ci * SCAN_CHUNK, SCAN_CHUNK)
        rows = pl.ds(base, SCAN_CHUNK)
        for g in range(HGRN_GROUP):
            cols = slice(g * HEAD_W, (g + 1) * HEAD_W)
            lb = lb_ref[g]
            f = lb + (1.0 - lb) * _sigmoid(fl_ref[0, rows, cols])
            hq = hq_ref[0, rows, cols]
            q = hq * _sigmoid(hq) * C_SCALE
            pieces = _hgrn_chunk(q, 1.0 - f, i_ref[0, rows, cols], jnp.log(f), st_ref.at[g], reverse)
            for p, piece in enumerate(pieces):
                o_ref[0, pl.ds(pl.multiple_of(base + 8 * p, 8), 8), cols] = piece
        return carry

    lax.fori_loop(0, n_chunks, chunk, 0)


def _ret_kernel(rk_ref, rv_ref, rq_ref, dec_ref, o_ref, st_ref, *, reverse):
    @pl.when(pl.program_id(1) == 0)
    def _():
        st_ref[...] = jnp.zeros_like(st_ref)

    c = TOK_BLOCK
    row = lax.broadcasted_iota(jnp.int32, (c, c), 0)
    col = lax.broadcasted_iota(jnp.int32, (c, c), 1)
    lag = ((col - row) if reverse else (row - col)).astype(F32)
    t_idx = lax.broadcasted_iota(jnp.int32, (c, 1), 0)
    steps = ((c - t_idx) if reverse else (t_idx + 1)).astype(F32)
    for h in range(N_HEADS):
        cols = slice(h * HEAD_W, (h + 1) * HEAD_W)
        lg = -jnp.exp(dec_ref[h])
        decay = jnp.where(lag >= 0, jnp.exp(lag * lg[:, 0:1]), 0.0)
        cum = steps * lg
        tot = c * lg
        q = rq_ref[0, :, cols]
        k = rk_ref[0, :, cols] * D_SCALE
        v = rv_ref[0, :, cols]
        st = st_ref[h]
        o = lax.dot_general(q * jnp.exp(cum), st, NT_DIMS, preferred_element_type=F32)
        att = lax.dot_general(q, k, NT_DIMS, preferred_element_type=F32) * decay
        o_ref[0, :, cols] = o + jnp.dot(att, v, preferred_element_type=F32)
        kd = k * jnp.exp(tot - cum)
        st_ref[h] = st * jnp.exp(tot) + lax.dot_general(v, kd, TN_DIMS, preferred_element_type=F32)


def _scans(proj, lb, ret_decay):
    nb, t, _ = proj.shape
    n_blk = t // TOK_BLOCK
    out_shape = jax.ShapeDtypeStruct((nb, t, MIX_W), F32)
    dec = jnp.broadcast_to(ret_decay.reshape(2, N_HEADS, 1, 1), (2, N_HEADS, 1, HEAD_W)).astype(F32)
    group_w = HGRN_GROUP * HEAD_W
    outs = []
    for d in (0, 1):
        blk = _scan_block_index(n_blk, d == 1)
        col = lambda base, blk=blk: pl.BlockSpec(
            (1, TOK_BLOCK, group_w), lambda b, g, i: (b, blk(i), base * (MIX_W // group_w) + g))
        outs.append(pl.pallas_call(
            functools.partial(_hgrn_kernel, reverse=d == 1),
            grid=(nb, N_HEADS // HGRN_GROUP, n_blk),
            in_specs=[col(d), col(2), col(5),
                      pl.BlockSpec((HGRN_GROUP, 1, HEAD_W), lambda b, g, i: (g, 0, 0))],
            out_specs=pl.BlockSpec((1, TOK_BLOCK, group_w), lambda b, g, i, blk=blk: (b, blk(i), g)),
            out_shape=out_shape,
            scratch_shapes=[pltpu.VMEM((HGRN_GROUP, HEAD_W, HEAD_W), F32)],
            compiler_params=_cparams(("parallel", "parallel", "arbitrary")),
        )(proj, proj, proj, lb.reshape(N_HEADS, 1, HEAD_W)))
    for d in (0, 1):
        blk = _scan_block_index(n_blk, d == 1)
        col = lambda base, blk=blk: pl.BlockSpec((1, TOK_BLOCK, MIX_W), lambda b, i: (b, blk(i), base))
        outs.append(pl.pallas_call(
            functools.partial(_ret_kernel, reverse=d == 1),
            grid=(nb, n_blk),
            in_specs=[col(3), col(4), col(7),
                      pl.BlockSpec((N_HEADS, 1, HEAD_W), lambda b, i, d=d: (d, 0, 0))],
            out_specs=pl.BlockSpec((1, TOK_BLOCK, MIX_W), lambda b, i, blk=blk: (b, blk(i), 0)),
            out_shape=out_shape,
            scratch_shapes=[pltpu.VMEM((N_HEADS, HEAD_W, HEAD_W), F32)],
            compiler_params=_cparams(("parallel", "arbitrary")),
        )(proj, proj, proj, dec.reshape(2 * N_HEADS, 1, HEAD_W)))
    return outs


def _odd_finish_kernel(ohf_ref, ohb_ref, orf_ref, orb_ref, hg_ref, rg_ref, hgain_ref, rgain_ref,
                       a_ref, b_ref):
    oh = ohf_ref[0] + ohb_ref[0]
    orr = orf_ref[0] + orb_ref[0]
    gh = oh * _sigmoid(hg_ref[0])
    rg = rg_ref[0]
    for h in range(N_HEADS):
        cols = slice(h * HEAD_W, (h + 1) * HEAD_W)
        x = gh[:, cols]
        a_ref[0, :, cols] = (x * lax.rsqrt(jnp.mean(x * x, axis=-1, keepdims=True) + EPS)
                             * hgain_ref[...]).astype(a_ref.dtype)
        y = orr[:, cols]
        yc = y - jnp.mean(y, axis=-1, keepdims=True)
        yn = yc * lax.rsqrt(jnp.mean(yc * yc, axis=-1, keepdims=True) + EPS) * rgain_ref[...]
        r = rg[:, cols]
        b_ref[0, :, cols] = (yn * (r * _sigmoid(r))).astype(b_ref.dtype)


def _odd_finish(scan_outs, proj, hg_gain, ret_gain):
    nb, t, _ = scan_outs[0].shape
    o_spec = pl.BlockSpec((1, TOK_BLOCK, MIX_W), lambda b, i: (b, i, 0))
    return pl.pallas_call(
        _odd_finish_kernel,
        grid=(nb, t // TOK_BLOCK),
        in_specs=[o_spec, o_spec, o_spec, o_spec,
                  pl.BlockSpec((1, TOK_BLOCK, MIX_W), lambda b, i: (b, i, 6)),
                  pl.BlockSpec((1, TOK_BLOCK, MIX_W), lambda b, i: (b, i, 8)),
                  pl.BlockSpec((1, HEAD_W), lambda b, i: (0, 0)),
                  pl.BlockSpec((1, HEAD_W), lambda b, i: (0, 0))],
        out_specs=[o_spec, o_spec],
        out_shape=[jax.ShapeDtypeStruct((nb, t, MIX_W), BF16)] * 2,
        compiler_params=_cparams(("parallel", "parallel")),
    )(*scan_outs, proj, proj, hg_gain.reshape(1, HEAD_W), ret_gain.reshape(1, HEAD_W))


def _out_kernel(x_ref, a_ref, b_ref, wa_ref, wb_ref, g1_ref, gn_ref, sh_ref, sc_ref, wq_ref,
                xo_ref, h_ref, q_ref):
    y = (jnp.dot(a_ref[0], wa_ref[...], preferred_element_type=F32)
         + jnp.dot(b_ref[0], wb_ref[...], preferred_element_type=F32))
    x = x_ref[0] + g1_ref[0] * y
    xo_ref[0] = x
    h = x * lax.rsqrt(jnp.mean(x * x, axis=-1, keepdims=True) + EPS) * gn_ref[...]
    h = (h * (1.0 + sc_ref[0]) + sh_ref[0]).astype(BF16)
    h_ref[0] = h
    q_ref[0] = jnp.dot(h, wq_ref[...], preferred_element_type=F32).astype(q_ref.dtype)


def _out_proj(x, a, b, w_out, mod, gn, wq):
    nb, t, d = x.shape
    nq = wq.shape[1]
    blk = lambda w: pl.BlockSpec((1, TOK_BLOCK, w), lambda bb, i: (bb, i, 0))
    vec = lambda which: pl.BlockSpec((1, 1, d), _mod_row(which, nb))
    full = lambda r, c: pl.BlockSpec((r, c), lambda bb, i: (0, 0))
    w16 = w_out.astype(BF16)
    return pl.pallas_call(
        _out_kernel,
        grid=(nb, t // TOK_BLOCK),
        in_specs=[blk(d), blk(MIX_W), blk(MIX_W), full(MIX_W, d), full(MIX_W, d),
                  vec(2), full(1, d), vec(3), vec(4), full(d, nq)],
        out_specs=[blk(d), blk(d), blk(nq)],
        out_shape=[jax.ShapeDtypeStruct((nb, t, d), F32),
                   jax.ShapeDtypeStruct((nb, t, d), BF16),
                   jax.ShapeDtypeStruct((nb, t, nq), BF16)],
        compiler_params=_cparams(("parallel", "parallel")),
    )(x, a, b, w16[:MIX_W], w16[MIX_W:], mod, gn.reshape(1, d), mod, mod, wq.astype(BF16))


def _top_values(work, n):
    rows = []
    for _ in range(n):
        m = jnp.max(work, axis=0, keepdims=True)
        rows.append(m)
        work = jnp.where(work == m, -jnp.inf, work)
    return rows


def _peer_scores(q_ref, keys_ref, s1_ref, e1_ref, th_ref, e0_ref, top_ref):
    n_top = PK_TOPK + 1
    top_ref[...] = jnp.full(top_ref.shape, -jnp.inf, F32)
    for h in range(PEER_HEADS):
        scores, tops = [], []
        for side in range(2):
            slab = (2 * h + side) * N_KEYS
            s = lax.dot_general(keys_ref[h, side], q_ref[:, slab:slab + N_KEYS], NT_DIMS,
                                preferred_element_type=F32)
            vals = _top_values(s, n_top)
            for r in range(n_top):
                top_ref[side, r:r + 1, :] = vals[r]
            scores.append(s)
            tops.append(vals)
        t0, t1 = tops
        cands = [t0[0] + top_ref[1]]
        cands += [t0[a] + top_ref[1, 0:8, :] for a in range(1, 8)]
        cands += [top_ref[0, 8:, :] + t1[0]]
        best = _top_values(jnp.concatenate(cands, axis=0), n_top)
        z = jnp.zeros_like(best[0])
        for r in range(PK_TOPK):
            z = z + jnp.exp(best[r] - best[0])
        tau = 0.5 * (best[PK_TOPK - 1] + best[PK_TOPK])
        th_ref[h] = tau - scores[0]
        e0_ref[h] = jnp.exp(scores[0] - t0[0])
        s1_ref[h] = scores[1]
        e1_ref[h] = jnp.exp(scores[1] - t1[0]) / z


def _peer_gate_piece(piece, s1_ref, e1_ref, rows_ref, act_ref, w_ref):
    ct, jq = divmod(piece, N_KEYS // PEER_JW)
    lanes = slice(ct * N_KEYS, (ct + 1) * N_KEYS)
    jrows = slice(jq * PEER_JW, (jq + 1) * PEER_JW)
    accs = [jnp.zeros((PEER_JW, N_KEYS), F32) for _ in range(PEER_ROWS)]
    for h in range(PEER_HEADS):
        s1 = s1_ref[h, jrows, lanes]
        e1 = e1_ref[h, jrows, lanes]
        for il in range(PEER_ROWS):
            th = pltpu.repeat(rows_ref[0, h, il, :, lanes], PEER_JW // 8, axis=0)
            e0 = pltpu.repeat(rows_ref[1, h, il, :, lanes], PEER_JW // 8, axis=0)
            accs[il] = accs[il] + jnp.where(s1 >= th, e1 * e0, 0.0)
    for il in range(PEER_ROWS):
        rows = slice(il * N_KEYS + jq * PEER_JW, il * N_KEYS + (jq + 1) * PEER_JW)
        x = act_ref[rows, lanes]
        inner = x * (0.7978845608028654 + (0.7978845608028654 * 0.044715) * (x * x))
        w_ref[rows, lanes] = ((accs[il] * (0.5 * x)) * (1.0 + jnp.tanh(inner))).astype(w_ref.dtype)


def _peer_kernel(h_ref, q_ref, keys_ref, u_first_ref, u_next_ref, vt_ref, vt_last_ref, o_ref,
                 s1_ref, e1_ref, th_ref, e0_ref, top_ref, rows_ref, act_a, act_b, w_a, w_b, acc_ref):
    ec = pl.program_id(1)
    n_chunks = pl.num_programs(1)

    @pl.when(ec == 0)
    def _():
        _peer_scores(q_ref, keys_ref, s1_ref, e1_ref, th_ref, e0_ref, top_ref)
        act_a[...] = lax.dot_general(u_first_ref[...], h_ref[...], NT_DIMS, preferred_element_type=F32)
        w_b[...] = jnp.zeros_like(w_b)
        acc_ref[...] = jnp.zeros_like(acc_ref)

    first_key = pl.multiple_of(ec * PEER_ROWS, PEER_ROWS)
    for which, src_ref in enumerate((th_ref, e0_ref)):
        for h in range(PEER_HEADS):
            group = src_ref[h, pl.ds(first_key, PEER_ROWS), :]
            for il in range(PEER_ROWS):
                rows_ref[which, h, il] = jnp.broadcast_to(group[il:il + 1], (8, group.shape[1]))

    n_gate = (PEER_TOK // N_KEYS) * (N_KEYS // PEER_JW)
    n_split = 4

    def matmul_piece(k, act_next, w_prev):
        if k < n_split:
            r = slice(k * (acc_ref.shape[0] // n_split), (k + 1) * (acc_ref.shape[0] // n_split))
            acc_ref[r, :] += jnp.dot(vt_ref[r, :], w_prev[...], preferred_element_type=F32)
        else:
            k -= n_split
            r = slice(k * (PEER_EXP // n_split), (k + 1) * (PEER_EXP // n_split))
            act_next[r, :] = lax.dot_general(u_next_ref[r, :], h_ref[...], NT_DIMS,
                                             preferred_element_type=F32)

    def step(act_cur, act_next, w_cur, w_prev):
        for piece in range(n_gate):
            if piece % (n_gate // (2 * n_split)) == 0:
                matmul_piece(piece // (n_gate // (2 * n_split)), act_next, w_prev)
            _peer_gate_piece(piece, s1_ref, e1_ref, rows_ref, act_cur, w_cur)

    @pl.when(ec % 2 == 0)
    def _():
        step(act_a, act_b, w_a, w_b)

    @pl.when(ec % 2 == 1)
    def _():
        step(act_b, act_a, w_b, w_a)

    @pl.when(ec == n_chunks - 1)
    def _():
        w_last = w_b
        acc = acc_ref[...] + jnp.dot(vt_last_ref[...], w_last[...], preferred_element_type=F32)
        o_ref[...] = acc.T


def _peer(h2, q, sub_keys, u_tab, v_tab):
    n, d = h2.shape
    n_exp = u_tab.shape[0]
    n_chunks = n_exp // PEER_EXP
    assert n_chunks % 2 == 0
    nq = q.shape[1]
    heads_tile = pltpu.VMEM((PEER_HEADS, N_KEYS, PEER_TOK), F32)
    act_tile = pltpu.VMEM((PEER_EXP, PEER_TOK), F32)
    w_tile = pltpu.VMEM((PEER_EXP, PEER_TOK), BF16)
    u16 = u_tab.astype(BF16)
    vt = v_tab.astype(BF16).T
    return pl.pallas_call(
        _peer_kernel,
        grid=(n // PEER_TOK, n_chunks),
        in_specs=[
            pl.BlockSpec((PEER_TOK, d), lambda t, e: (t, 0)),
            pl.BlockSpec((PEER_TOK, nq), lambda t, e: (t, 0)),
            pl.BlockSpec((PEER_HEADS, 2, N_KEYS, N_KEYS), lambda t, e: (0, 0, 0, 0)),
            pl.BlockSpec((PEER_EXP, d), lambda t, e: (0, 0)),
            pl.BlockSpec((PEER_EXP, d), lambda t, e: (jnp.minimum(e + 1, n_chunks - 1), 0)),
            pl.BlockSpec((d, PEER_EXP), lambda t, e: (0, jnp.maximum(e - 1, 0))),
            pl.BlockSpec((d, PEER_EXP), lambda t, e: (0, n_chunks - 1)),
        ],
        out_specs=pl.BlockSpec((PEER_TOK, d), lambda t, e: (t, 0)),
        out_shape=jax.ShapeDtypeStruct((n, d), F32),
        scratch_shapes=[heads_tile, heads_tile, heads_tile, heads_tile,
                        pltpu.VMEM((2, 24, PEER_TOK), F32),
                        pltpu.VMEM((2, PEER_HEADS, PEER_ROWS, 8, PEER_TOK), F32),
                        act_tile, act_tile, w_tile, w_tile,
                        pltpu.VMEM((d, PEER_TOK), F32)],
        compiler_params=_cparams(("parallel", "arbitrary")),
    )(h2, q, sub_keys.astype(BF16), u16, u16, vt, vt)


def _final_kernel(x_ref, p_ref, g2_ref, g_ref, o_ref):
    x = x_ref[0] + g2_ref[0] * p_ref[0]
    o_ref[0] = x * lax.rsqrt(jnp.mean(x * x, axis=-1, keepdims=True) + EPS) * g_ref[...]


def _final_norm(x, peer, mod, final_g, n_ctx):
    nb, t, d = x.shape
    skip = n_ctx // TOK_BLOCK
    blk = pl.BlockSpec((1, TOK_BLOCK, d), lambda b, i: (b, i + skip, 0))
    return pl.pallas_call(
        _final_kernel,
        grid=(nb, (t - n_ctx) // TOK_BLOCK),
        in_specs=[blk, blk,
                  pl.BlockSpec((1, 1, d), lambda b, i: (b * 6 + 5, 0, 0)),
                  pl.BlockSpec((1, d), lambda b, i: (0, 0))],
        out_specs=pl.BlockSpec((1, TOK_BLOCK, d), lambda b, i: (b, i, 0)),
        out_shape=jax.ShapeDtypeStruct((nb, t - n_ctx, d), F32),
        compiler_params=_cparams(("parallel", "parallel")),
    )(x, peer, mod, final_g.reshape(1, d))


def _rope_tables(n_ctx, n_tok, half):
    lane = jnp.arange(HEAD_W)
    part_w = 2 * half
    inv = ROPE_BASE ** (-(lane % half).astype(F32) / half)
    use_col = (lane // part_w) % 2 == 1
    tok = jnp.arange(n_tok)
    pos = jnp.where(use_col[None, :], (tok % GRID_W)[:, None], (tok // GRID_W)[:, None]).astype(F32)
    ang = pos * inv[None, :]
    sign = jnp.where((lane % part_w) < half, -1.0, 1.0)
    cos = jnp.concatenate([jnp.ones((n_ctx, HEAD_W), F32), jnp.cos(ang)], axis=0)
    sin = jnp.concatenate([jnp.zeros((n_ctx, HEAD_W), F32), jnp.sin(ang) * sign[None, :]], axis=0)
    return cos, sin


def kernel(x, c, ctx, c_ctx, w_ada, b_ada, norm_g, w_in_even, gmlp_ws, gmlp_bs, gmlp_v_gain, diff_lambda, diff_sub_gain, w_in_odd, hgrn_lower_bounds, hgrn_gain, ret_log_decay, ret_gain, w_out, peer_wq, peer_sub_keys, peer_u, peer_v, final_g):
    nb, n_tok, d = x.shape
    n_ctx = ctx.shape[1]
    depth = w_ada.shape[0]
    assert d == D_MODEL and n_ctx == TOK_BLOCK and n_tok % TOK_BLOCK == 0 and nb < 8
    assert (nb * (n_ctx + n_tok)) % PEER_TOK == 0
    t = n_ctx + n_tok

    stream = jnp.concatenate([ctx, x], axis=1)
    cvec = jnp.zeros((8, d), F32).at[:nb].set(c).at[nb].set(c_ctx)
    mod_all = _ada_mod(cvec, w_ada, b_ada).reshape(depth, 8 * 6, 1, d)

    lb_all = jnp.cumsum(jax.nn.softmax(hgrn_lower_bounds.astype(F32), axis=0), axis=0)
    lb_all = lb_all - lb_all[0]
    cos_e, sin_e = _rope_tables(n_ctx, n_tok, 16)
    cos_o, sin_o = _rope_tables(n_ctx, n_tok, 32)

    peer_out = None
    prev_mod = None
    for i in range(depth):
        j = i // 2
        mod = mod_all[i]
        if i % 2 == 0:
            lam_init = 0.8 - 0.6 * math.exp(-0.3 * i)
            proj, stream = _proj(stream, peer_out, prev_mod, mod, norm_g[i, 0], w_in_even[j].astype(BF16),
                                 cos_e, sin_e, rope_slabs=tuple(range(0, 4)) + tuple(range(16, 20)),
                                 rope_shift=16, out_dtype=BF16)
            a = _gmlp(proj, gmlp_ws[j], gmlp_bs[j], gmlp_v_gain[j])
            b = _diff_attention(proj, diff_lambda[j].astype(F32), diff_sub_gain[j], lam_init)
        else:
            proj, stream = _proj(stream, peer_out, prev_mod, mod, norm_g[i, 0], w_in_odd[j].astype(BF16),
                                 cos_o, sin_o, rope_slabs=tuple(range(12, 16)) + tuple(range(28, 32)),
                                 rope_shift=32, out_dtype=F32)
            a, b = _odd_finish(_scans(proj, lb_all[i], ret_log_decay[j]), proj, hgrn_gain[j], ret_gain[j])
        stream, h2, q = _out_proj(stream, a, b, w_out[i], mod, norm_g[i, 1], peer_wq[i])
        peer_out = _peer(h2.reshape(nb * t, d), q.reshape(nb * t, -1), peer_sub_keys[i],
                         peer_u[i], peer_v[i]).reshape(nb, t, d)
        prev_mod = mod
    return _final_norm(stream, peer_out, prev_mod, final_g, n_ctx)
```

```python
import functools
import math

import jax
import jax.numpy as jnp
from jax import lax
from jax.experimental import pallas as pl
from jax.experimental.pallas import tpu as pltpu

F32 = jnp.float32
BF16 = jnp.bfloat16

D_MODEL = 1024
MIX_W = D_MODEL // 2
HEAD_W = 128
N_HEADS = MIX_W // HEAD_W
GRID_W = 64
GMLP_CHUNK = 128
SCAN_CHUNK = 64
HGRN_SUB = 16
HGRN_GROUP = 4
PEER_HEADS = 8
N_KEYS = 128
PK_TOPK = 16
ROPE_BASE = 10000.0
EPS = 1e-6
DIFF_SCALE = 64 ** -0.5
C_SCALE = HEAD_W ** -0.5
D_SCALE = HEAD_W ** -0.5

TOK_BLOCK = 256
ATTN_KV_TILE = 768
PEER_TOK = 512
PEER_EXP = 1024
PEER_ROWS = PEER_EXP // N_KEYS
PEER_JW = 32
VMEM_LIMIT = 56 * 1024 * 1024

NT_DIMS = (((1,), (1,)), ((), ()))
TN_DIMS = (((0,), (0,)), ((), ()))


def _cparams(sem):
    return pltpu.CompilerParams(dimension_semantics=sem, vmem_limit_bytes=VMEM_LIMIT)


def _gelu(x):
    return 0.5 * x * (1.0 + jnp.tanh(0.7978845608028654 * (x + 0.044715 * x * x * x)))


def _sigmoid(x):
    return 1.0 / (1.0 + jnp.exp(-x))


def _mod_row(which, n_batch):
    def index(b, t):
        return (jnp.where(t == 0, n_batch, b) * 6 + which, 0, 0)
    return index


def _ada_kernel(c_ref, w_ref, b_ref, o_ref):
    c = c_ref[...]
    a = (c * _sigmoid(c)).astype(BF16)
    o_ref[0] = jnp.dot(a, w_ref[0].astype(BF16), preferred_element_type=F32) + b_ref[0]


def _ada_mod(cvec, w_ada, b_ada):
    depth, d, n = w_ada.shape
    tn = 1536
    return pl.pallas_call(
        _ada_kernel,
        grid=(depth, n // tn),
        in_specs=[
            pl.BlockSpec((8, d), lambda i, j: (0, 0)),
            pl.BlockSpec((1, d, tn), lambda i, j: (i, 0, j)),
            pl.BlockSpec((1, 1, tn), lambda i, j: (i, 0, j)),
        ],
        out_specs=pl.BlockSpec((1, 8, tn), lambda i, j: (i, 0, j)),
        out_shape=jax.ShapeDtypeStruct((depth, 8, n), F32),
        compiler_params=_cparams(("parallel", "parallel")),
    )(cvec, w_ada, b_ada.reshape(depth, 1, n))


def _rope_slab(seg, cos, sin, shift):
    lane = lax.broadcasted_iota(jnp.int32, seg.shape, 1)
    first = (lane % (2 * shift)) < shift
    partner = jnp.where(first, pltpu.roll(seg, HEAD_W - shift, 1), pltpu.roll(seg, shift, 1))
    return seg * cos + partner * sin


def _proj_kernel(*refs, rope_slabs, rope_shift, add_peer):
    if add_peer:
        x_ref, p_ref, g2_ref, gn_ref, sh_ref, sc_ref, w_ref, cos_ref, sin_ref, o_ref, s_ref = refs
        x = x_ref[0] + g2_ref[0] * p_ref[0]
        s_ref[0] = x
    else:
        x_ref, gn_ref, sh_ref, sc_ref, w_ref, cos_ref, sin_ref, o_ref = refs
        x = x_ref[0]
    h = x * lax.rsqrt(jnp.mean(x * x, axis=-1, keepdims=True) + EPS) * gn_ref[...]
    h = h * (1.0 + sc_ref[0]) + sh_ref[0]
    acc = jnp.dot(h.astype(BF16), w_ref[...], preferred_element_type=F32)
    o_ref[0] = acc.astype(o_ref.dtype)
    cos = cos_ref[...]
    sin = sin_ref[...]
    for c in rope_slabs:
        seg = acc[:, c * HEAD_W:(c + 1) * HEAD_W]
        o_ref[0, :, c * HEAD_W:(c + 1) * HEAD_W] = _rope_slab(seg, cos, sin, rope_shift).astype(o_ref.dtype)


def _proj(x, peer, prev_mod, mod, gn, w, cos, sin, *, rope_slabs, rope_shift, out_dtype):
    nb, t, d = x.shape
    n = w.shape[1]
    blk = pl.BlockSpec((1, TOK_BLOCK, d), lambda b, i: (b, i, 0))
    vec = lambda which: pl.BlockSpec((1, 1, d), _mod_row(which, nb))
    in_specs = [blk]
    args = [x]
    if peer is not None:
        in_specs += [blk, vec(5)]
        args += [peer, prev_mod]
    in_specs += [
        pl.BlockSpec((1, d), lambda b, i: (0, 0)),
        vec(0), vec(1),
        pl.BlockSpec((d, n), lambda b, i: (0, 0)),
        pl.BlockSpec((TOK_BLOCK, HEAD_W), lambda b, i: (i, 0)),
        pl.BlockSpec((TOK_BLOCK, HEAD_W), lambda b, i: (i, 0)),
    ]
    args += [gn.reshape(1, d), mod, mod, w, cos, sin]
    out_specs = [pl.BlockSpec((1, TOK_BLOCK, n), lambda b, i: (b, i, 0))]
    out_shape = [jax.ShapeDtypeStruct((nb, t, n), out_dtype)]
    if peer is not None:
        out_specs.append(blk)
        out_shape.append(jax.ShapeDtypeStruct((nb, t, d), F32))
    res = pl.pallas_call(
        functools.partial(_proj_kernel, rope_slabs=rope_slabs, rope_shift=rope_shift,
                          add_peer=peer is not None),
        grid=(nb, t // TOK_BLOCK),
        in_specs=in_specs,
        out_specs=out_specs,
        out_shape=out_shape,
        compiler_params=_cparams(("parallel", "parallel")),
    )(*args)
    return (res[0], res[1]) if peer is not None else (res[0], x)


def _attn_kernel(lv_ref, q_ref, k_ref, v_ref, sg_ref, o_ref, s_a, s_b, *, n_kv, kv_tile, lam_init):
    qi = pl.program_id(2)
    q = q_ref[0]
    lane = lax.broadcasted_iota(jnp.int32, q.shape, 1)
    qs = q * jnp.asarray(DIFF_SCALE, q.dtype)
    zero = jnp.zeros_like(qs)
    q1 = jnp.where(lane < HEAD_W // 2, qs, zero)
    q2 = jnp.where(lane < HEAD_W // 2, zero, qs)
    tq = q.shape[0]
    qq = jnp.concatenate([q1, q2], axis=0)

    def scores(rows):
        return lax.dot_general(qq, k_ref[0, rows, :], NT_DIMS, preferred_element_type=F32)

    def update(s, rows, carry):
        m, l, a = carry
        v = v_ref[0, rows, :]
        m_new = jnp.maximum(m, jnp.max(s, axis=-1, keepdims=True))
        alpha = jnp.exp(m - m_new)
        p = jnp.exp(s - m_new)
        l_new = alpha * l + jnp.sum(p, axis=-1, keepdims=True)
        a_new = alpha * a + jnp.dot(p.astype(v.dtype), v, preferred_element_type=F32)
        return m_new, l_new, a_new

    def tile(c):
        return pl.ds(pl.multiple_of(c * kv_tile, kv_tile), kv_tile)

    def finish(carry):
        m, l, a = carry
        lv = lv_ref[...]
        lam = (jnp.exp(jnp.sum(lv[0:1] * lv[1:2], axis=-1, keepdims=True))
               - jnp.exp(jnp.sum(lv[2:3] * lv[3:4], axis=-1, keepdims=True)) + lam_init)
        o = a / l
        o = o[:tq] - lam * o[tq:]
        o = o * lax.rsqrt(jnp.mean(o * o, axis=-1, keepdims=True) + EPS) * sg_ref[...]
        o_ref[0] = (o * (1.0 - lam_init)).astype(o_ref.dtype)

    init = (jnp.full((2 * tq, 1), -jnp.inf, F32), jnp.zeros((2 * tq, 1), F32),
            jnp.zeros((2 * tq, HEAD_W), F32))

    @pl.when(qi == 0)
    def _():
        ctx_rows = pl.ds(0, TOK_BLOCK)
        finish(update(scores(ctx_rows), ctx_rows, init))

    @pl.when(qi > 0)
    def _():
        n_pairs = (n_kv - 1) // 2
        s_a[...] = scores(tile(0))

        def pair(i, carry):
            c = 2 * i
            s_b[...] = scores(tile(c + 1))
            carry = update(s_a[...], tile(c), carry)
            s_a[...] = scores(tile(c + 2))
            return update(s_b[...], tile(c + 1), carry)

        carry = lax.fori_loop(0, n_pairs, pair, init)
        last = n_kv - 1
        if last == 2 * n_pairs:
            carry = update(s_a[...], tile(last), carry)
        else:
            s_b[...] = scores(tile(last))
            carry = update(s_a[...], tile(last - 1), carry)
            carry = update(s_b[...], tile(last), carry)
        finish(carry)


def _diff_attention(proj, lam_vecs, sub_gain, lam_init):
    nb, t, _ = proj.shape
    n_q = t // TOK_BLOCK
    kv_tile = ATTN_KV_TILE if t % ATTN_KV_TILE == 0 else TOK_BLOCK
    return pl.pallas_call(
        functools.partial(_attn_kernel, n_kv=t // kv_tile, kv_tile=kv_tile, lam_init=lam_init),
        grid=(nb, N_HEADS, n_q),
        in_specs=[
            pl.BlockSpec((4, HEAD_W // 2), lambda b, h, i: (0, 0)),
            pl.BlockSpec((1, TOK_BLOCK, HEAD_W), lambda b, h, i: (b, i, 4 * N_HEADS + h)),
            pl.BlockSpec((1, t, HEAD_W), lambda b, h, i: (b, 0, h)),
            pl.BlockSpec((1, t, HEAD_W), lambda b, h, i: (b, 0, N_HEADS + h)),
            pl.BlockSpec((1, HEAD_W), lambda b, h, i: (0, 0)),
        ],
        out_specs=pl.BlockSpec((1, TOK_BLOCK, HEAD_W), lambda b, h, i: (b, i, h)),
        out_shape=jax.ShapeDtypeStruct((nb, t, MIX_W), BF16),
        scratch_shapes=[pltpu.VMEM((2 * TOK_BLOCK, kv_tile), F32)] * 2,
        compiler_params=_cparams(("parallel", "parallel", "arbitrary")),
    )(lam_vecs, proj, proj, proj, sub_gain.reshape(1, HEAD_W))


def _gmlp_kernel(u_ref, v_ref, ws_ref, bs_ref, vg_ref, o_ref):
    u = _gelu(u_ref[0].astype(F32))
    v = _gelu(v_ref[0].astype(F32))
    vc = v - jnp.mean(v, axis=-1, keepdims=True)
    vn = vc * lax.rsqrt(jnp.mean(vc * vc, axis=-1, keepdims=True) + EPS) * vg_ref[...]
    vn = vn.astype(BF16)
    for ch in range(TOK_BLOCK // GMLP_CHUNK):
        rows = slice(ch * GMLP_CHUNK, (ch + 1) * GMLP_CHUNK)
        for g in range(N_HEADS):
            cols = slice(g * HEAD_W, (g + 1) * HEAD_W)
            z = jnp.dot(ws_ref[g], vn[rows, cols], preferred_element_type=F32) + bs_ref[g]
            o_ref[0, rows, cols] = (u[rows, cols] * z).astype(o_ref.dtype)


def _gmlp(proj, ws, bs, v_gain):
    nb, t, _ = proj.shape
    bs_b = jnp.broadcast_to(bs[:, :, None], (N_HEADS, GMLP_CHUNK, HEAD_W)).astype(F32)
    return pl.pallas_call(
        _gmlp_kernel,
        grid=(nb, t // TOK_BLOCK),
        in_specs=[
            pl.BlockSpec((1, TOK_BLOCK, MIX_W), lambda b, i: (b, i, 2)),
            pl.BlockSpec((1, TOK_BLOCK, MIX_W), lambda b, i: (b, i, 3)),
            pl.BlockSpec((N_HEADS, GMLP_CHUNK, GMLP_CHUNK), lambda b, i: (0, 0, 0)),
            pl.BlockSpec((N_HEADS, GMLP_CHUNK, HEAD_W), lambda b, i: (0, 0, 0)),
            pl.BlockSpec((1, MIX_W), lambda b, i: (0, 0)),
        ],
        out_specs=pl.BlockSpec((1, TOK_BLOCK, MIX_W), lambda b, i: (b, i, 0)),
        out_shape=jax.ShapeDtypeStruct((nb, t, MIX_W), BF16),
        compiler_params=_cparams(("parallel", "parallel")),
    )(proj, proj, ws.astype(BF16), bs_b, v_gain.reshape(1, MIX_W))


def _scan_block_index(n_blk, reverse):
    if not reverse:
        return lambda i: i
    return lambda i: jnp.where(i == 0, 0, n_blk - i)


def _hgrn_chunk(q, k, v, logf, st_ref, reverse):
    c = q.shape[0]
    row = lax.broadcasted_iota(jnp.int32, (c, c), 0)
    col = lax.broadcasted_iota(jnp.int32, (c, c), 1)
    tri = ((col >= row) if reverse else (col <= row)).astype(F32)
    cum = jnp.dot(tri, logf, preferred_element_type=F32, precision=lax.Precision.HIGHEST)
    tot = jnp.sum(logf, axis=0, keepdims=True)
    st = st_ref[...]
    o_inter = lax.dot_general(q * jnp.exp(cum), st, NT_DIMS, preferred_element_type=F32)
    t8 = lax.broadcasted_iota(jnp.int32, (8, 1), 0)
    pieces = []
    for blk in range(c // HGRN_SUB):
        lo = blk * HGRN_SUB
        rows = slice(lo, lo + HGRN_SUB)
        o_blk = o_inter[rows]
        earlier = slice(lo + HGRN_SUB, c) if reverse else slice(0, lo)
        if earlier.stop > earlier.start:
            edge = lo + HGRN_SUB if reverse else lo - 1
            ref = cum[edge:edge + 1]
            qt = q[rows] * jnp.exp(cum[rows] - ref)
            kt = k[earlier] * jnp.exp(ref - cum[earlier])
            att = lax.dot_general(qt, kt, NT_DIMS, preferred_element_type=F32)
            o_blk = o_blk + jnp.dot(att, v[earlier], preferred_element_type=F32)
        halves = [o_blk[0:8], o_blk[8:16]]
        for sl in range(HGRN_SUB):
            s = lo + sl
            needed = [hh for hh in range(2) if ((8 * hh <= sl) if reverse else (8 * hh + 7 >= sl))]
            for hh in needed:
                r = slice(lo + 8 * hh, lo + 8 * hh + 8)
                w = q[r] * jnp.exp(jnp.minimum(cum[r] - cum[s:s + 1], 0.0)) * k[s:s + 1]
                a = jnp.sum(w, axis=-1, keepdims=True)
                t_loc = t8 + 8 * hh
                valid = (t_loc <= sl) if reverse else (t_loc >= sl)
                halves[hh] = halves[hh] + jnp.where(valid, a, 0.0) * v[s:s + 1]
        pieces += halves
    kd = k * jnp.exp(tot - cum)
    st_ref[...] = st * jnp.exp(tot) + lax.dot_general(v, kd, TN_DIMS, preferred_element_type=F32)
    return pieces


def _hgrn_kernel(fl_ref, i_ref, hq_ref, lb_ref, o_ref, st_ref, *, reverse):
    @pl.when(pl.program_id(2) == 0)
    def _():
        st_ref[...] = jnp.zeros_like(st_ref)

    n_chunks = TOK_BLOCK // SCAN_CHUNK

    def chunk(j, carry):
        ci = n_chunks - 1 - j if reverse else j
        base = pl.multiple_of(ci * SCAN_CHUNK, SCAN_CHUNK)
        rows = pl.ds(base, SCAN_CHUNK)
        for g in range(HGRN_GROUP):
            cols = slice(g * HEAD_W, (g + 1) * HEAD_W)
            lb = lb_ref[g]
            f = lb + (1.0 - lb) * _sigmoid(fl_ref[0, rows, cols])
            hq = hq_ref[0, rows, cols]
            q = hq * _sigmoid(hq) * C_SCALE
            pieces = _hgrn_chunk(q, 1.0 - f, i_ref[0, rows, cols], jnp.log(f), st_ref.at[g], reverse)
            for p, piece in enumerate(pieces):
                o_ref[0, pl.ds(pl.multiple_of(base + 8 * p, 8), 8), cols] = piece
        return carry

    lax.fori_loop(0, n_chunks, chunk, 0)


def _ret_kernel(rk_ref, rv_ref, rq_ref, dec_ref, o_ref, st_ref, *, reverse):
    @pl.when(pl.program_id(1) == 0)
    def _():
        st_ref[...] = jnp.zeros_like(st_ref)

    c = TOK_BLOCK
    row = lax.broadcasted_iota(jnp.int32, (c, c), 0)
    col = lax.broadcasted_iota(jnp.int32, (c, c), 1)
    lag = ((col - row) if reverse else (row - col)).astype(F32)
    t_idx = lax.broadcasted_iota(jnp.int32, (c, 1), 0)
    steps = ((c - t_idx) if reverse else (t_idx + 1)).astype(F32)
    for h in range(N_HEADS):
        cols = slice(h * HEAD_W, (h + 1) * HEAD_W)
        lg = -jnp.exp(dec_ref[h])
        decay = jnp.where(lag >= 0, jnp.exp(lag * lg[:, 0:1]), 0.0)
        cum = steps * lg
        tot = c * lg
        q = rq_ref[0, :, cols]
        k = rk_ref[0, :, cols] * D_SCALE
        v = rv_ref[0, :, cols]
        st = st_ref[h]
        o = lax.dot_general(q * jnp.exp(cum), st, NT_DIMS, preferred_element_type=F32)
        att = lax.dot_general(q, k, NT_DIMS, preferred_element_type=F32) * decay
        o_ref[0, :, cols] = o + jnp.dot(att, v, preferred_element_type=F32)
        kd = k * jnp.exp(tot - cum)
        st_ref[h] = st * jnp.exp(tot) + lax.dot_general(v, kd, TN_DIMS, preferred_element_type=F32)


def _scans(proj, lb, ret_decay):
    nb, t, _ = proj.shape
    n_blk = t // TOK_BLOCK
    out_shape = jax.ShapeDtypeStruct((nb, t, MIX_W), F32)
    dec = jnp.broadcast_to(ret_decay.reshape(2, N_HEADS, 1, 1), (2, N_HEADS, 1, HEAD_W)).astype(F32)
    group_w = HGRN_GROUP * HEAD_W
    outs = []
    for d in (0, 1):
        blk = _scan_block_index(n_blk, d == 1)
        col = lambda base, blk=blk: pl.BlockSpec(
            (1, TOK_BLOCK, group_w), lambda b, g, i: (b, blk(i), base * (MIX_W // group_w) + g))
        outs.append(pl.pallas_call(
            functools.partial(_hgrn_kernel, reverse=d == 1),
            grid=(nb, N_HEADS // HGRN_GROUP, n_blk),
            in_specs=[col(d), col(2), col(5),
                      pl.BlockSpec((HGRN_GROUP, 1, HEAD_W), lambda b, g, i: (g, 0, 0))],
            out_specs=pl.BlockSpec((1, TOK_BLOCK, group_w), lambda b, g, i, blk=blk: (b, blk(i), g)),
            out_shape=out_shape,
            scratch_shapes=[pltpu.VMEM((HGRN_GROUP, HEAD_W, HEAD_W), F32)],
            compiler_params=_cparams(("parallel", "parallel", "arbitrary")),
        )(proj, proj, proj, lb.reshape(N_HEADS, 1, HEAD_W)))
    for d in (0, 1):
        blk = _scan_block_index(n_blk, d == 1)
        col = lambda base, blk=blk: pl.BlockSpec((1, TOK_BLOCK, MIX_W), lambda b, i: (b, blk(i), base))
        outs.append(pl.pallas_call(
            functools.partial(_ret_kernel, reverse=d == 1),
            grid=(nb, n_blk),
            in_specs=[col(3), col(4), col(7),
                      pl.BlockSpec((N_HEADS, 1, HEAD_W), lambda b, i, d=d: (d, 0, 0))],
            out_specs=pl.BlockSpec((1, TOK_BLOCK, MIX_W), lambda b, i, blk=blk: (b, blk(i), 0)),
            out_shape=out_shape,
            scratch_shapes=[pltpu.VMEM((N_HEADS, HEAD_W, HEAD_W), F32)],
            compiler_params=_cparams(("parallel", "arbitrary")),
        )(proj, proj, proj, dec.reshape(2 * N_HEADS, 1, HEAD_W)))
    return outs


def _odd_finish_kernel(ohf_ref, ohb_ref, orf_ref, orb_ref, hg_ref, rg_ref, hgain_ref, rgain_ref,
                       a_ref, b_ref):
    oh = ohf_ref[0] + ohb_ref[0]
    orr = orf_ref[0] + orb_ref[0]
    gh = oh * _sigmoid(hg_ref[0])
    rg = rg_ref[0]
    for h in range(N_HEADS):
        cols = slice(h * HEAD_W, (h + 1) * HEAD_W)
        x = gh[:, cols]
        a_ref[0, :, cols] = (x * lax.rsqrt(jnp.mean(x * x, axis=-1, keepdims=True) + EPS)
                             * hgain_ref[...]).astype(a_ref.dtype)
        y = orr[:, cols]
        yc = y - jnp.mean(y, axis=-1, keepdims=True)
        yn = yc * lax.rsqrt(jnp.mean(yc * yc, axis=-1, keepdims=True) + EPS) * rgain_ref[...]
        r = rg[:, cols]
        b_ref[0, :, cols] = (yn * (r * _sigmoid(r))).astype(b_ref.dtype)


def _odd_finish(scan_outs, proj, hg_gain, ret_gain):
    nb, t, _ = scan_outs[0].shape
    o_spec = pl.BlockSpec((1, TOK_BLOCK, MIX_W), lambda b, i: (b, i, 0))
    return pl.pallas_call(
        _odd_finish_kernel,
        grid=(nb, t // TOK_BLOCK),
        in_specs=[o_spec, o_spec, o_spec, o_spec,
                  pl.BlockSpec((1, TOK_BLOCK, MIX_W), lambda b, i: (b, i, 6)),
                  pl.BlockSpec((1, TOK_BLOCK, MIX_W), lambda b, i: (b, i, 8)),
                  pl.BlockSpec((1, HEAD_W), lambda b, i: (0, 0)),
                  pl.BlockSpec((1, HEAD_W), lambda b, i: (0, 0))],
        out_specs=[o_spec, o_spec],
        out_shape=[jax.ShapeDtypeStruct((nb, t, MIX_W), BF16)] * 2,
        compiler_params=_cparams(("parallel", "parallel")),
    )(*scan_outs, proj, proj, hg_gain.reshape(1, HEAD_W), ret_gain.reshape(1, HEAD_W))


def _out_kernel(x_ref, a_ref, b_ref, wa_ref, wb_ref, g1_ref, gn_ref, sh_ref, sc_ref, wq_ref,
                xo_ref, h_ref, q_ref):
    y = (jnp.dot(a_ref[0], wa_ref[...], preferred_element_type=F32)
         + jnp.dot(b_ref[0], wb_ref[...], preferred_element_type=F32))
    x = x_ref[0] + g1_ref[0] * y
    xo_ref[0] = x
    h = x * lax.rsqrt(jnp.mean(x * x, axis=-1, keepdims=True) + EPS) * gn_ref[...]
    h = (h * (1.0 + sc_ref[0]) + sh_ref[0]).astype(BF16)
    h_ref[0] = h
    q_ref[0] = jnp.dot(h, wq_ref[...], preferred_element_type=F32).astype(q_ref.dtype)


def _out_proj(x, a, b, w_out, mod, gn, wq):
    nb, t, d = x.shape
    nq = wq.shape[1]
    blk = lambda w: pl.BlockSpec((1, TOK_BLOCK, w), lambda bb, i: (bb, i, 0))
    vec = lambda which: pl.BlockSpec((1, 1, d), _mod_row(which, nb))
    full = lambda r, c: pl.BlockSpec((r, c), lambda bb, i: (0, 0))
    w16 = w_out.astype(BF16)
    return pl.pallas_call(
        _out_kernel,
        grid=(nb, t // TOK_BLOCK),
        in_specs=[blk(d), blk(MIX_W), blk(MIX_W), full(MIX_W, d), full(MIX_W, d),
                  vec(2), full(1, d), vec(3), vec(4), full(d, nq)],
        out_specs=[blk(d), blk(d), blk(nq)],
        out_shape=[jax.ShapeDtypeStruct((nb, t, d), F32),
                   jax.ShapeDtypeStruct((nb, t, d), BF16),
                   jax.ShapeDtypeStruct((nb, t, nq), BF16)],
        compiler_params=_cparams(("parallel", "parallel")),
    )(x, a, b, w16[:MIX_W], w16[MIX_W:], mod, gn.reshape(1, d), mod, mod, wq.astype(BF16))


def _top_values(work, n):
    rows = []
    for _ in range(n):
        m = jnp.max(work, axis=0, keepdims=True)
        rows.append(m)
        work = jnp.where(work == m, -jnp.inf, work)
    return rows


def _peer_scores(q_ref, keys_ref, s1_ref, e1_ref, th_ref, e0_ref, top_ref):
    n_top = PK_TOPK + 1
    top_ref[...] = jnp.full(top_ref.shape, -jnp.inf, F32)
    for h in range(PEER_HEADS):
        scores, tops = [], []
        for side in range(2):
            slab = (2 * h + side) * N_KEYS
            s = lax.dot_general(keys_ref[h, side], q_ref[:, slab:slab + N_KEYS], NT_DIMS,
                                preferred_element_type=F32)
            vals = _top_values(s, n_top)
            for r in range(n_top):
                top_ref[side, r:r + 1, :] = vals[r]
            scores.append(s)
            tops.append(vals)
        t0, t1 = tops
        cands = [t0[0] + top_ref[1]]
        cands += [t0[a] + top_ref[1, 0:8, :] for a in range(1, 8)]
        cands += [top_ref[0, 8:, :] + t1[0]]
        best = _top_values(jnp.concatenate(cands, axis=0), n_top)
        z = jnp.zeros_like(best[0])
        for r in range(PK_TOPK):
            z = z + jnp.exp(best[r] - best[0])
        tau = 0.5 * (best[PK_TOPK - 1] + best[PK_TOPK])
        th_ref[h] = tau - scores[0]
        e0_ref[h] = jnp.exp(scores[0] - t0[0])
        s1_ref[h] = scores[1]
        e1_ref[h] = jnp.exp(scores[1] - t1[0]) / z


def _peer_gate_piece(piece, s1_ref, e1_ref, rows_ref, act_ref, w_ref):
    ct, jq = divmod(piece, N_KEYS // PEER_JW)
    lanes = slice(ct * N_KEYS, (ct + 1) * N_KEYS)
    jrows = slice(jq * PEER_JW, (jq + 1) * PEER_JW)
    accs = [jnp.zeros((PEER_JW, N_KEYS), F32) for _ in range(PEER_ROWS)]
    for h in range(PEER_HEADS):
        s1 = s1_ref[h, jrows, lanes]
        e1 = e1_ref[h, jrows, lanes]
        for il in range(PEER_ROWS):
            th = jnp.concatenate([rows_ref[0, h, il, :, lanes]] * (PEER_JW // 8), axis=0)
            e0 = jnp.concatenate([rows_ref[1, h, il, :, lanes]] * (PEER_JW // 8), axis=0)
            accs[il] = accs[il] + jnp.where(s1 >= th, e1 * e0, 0.0)
    for il in range(PEER_ROWS):
        rows = slice(il * N_KEYS + jq * PEER_JW, il * N_KEYS + (jq + 1) * PEER_JW)
        x = act_ref[rows, lanes]
        inner = x * (0.7978845608028654 + (0.7978845608028654 * 0.044715) * (x * x))
        w_ref[rows, lanes] = ((accs[il] * (0.5 * x)) * (1.0 + jnp.tanh(inner))).astype(w_ref.dtype)


def _peer_kernel(h_ref, q_ref, keys_ref, u_first_ref, u_next_ref, vt_ref, vt_last_ref, o_ref,
                 s1_ref, e1_ref, th_ref, e0_ref, top_ref, rows_ref, act_a, act_b, w_a, w_b, acc_ref):
    ec = pl.program_id(1)
    n_chunks = pl.num_programs(1)

    @pl.when(ec == 0)
    def _():
        _peer_scores(q_ref, keys_ref, s1_ref, e1_ref, th_ref, e0_ref, top_ref)
        act_a[...] = lax.dot_general(u_first_ref[...], h_ref[...], NT_DIMS, preferred_element_type=F32)
        w_b[...] = jnp.zeros_like(w_b)
        acc_ref[...] = jnp.zeros_like(acc_ref)

    first_key = pl.multiple_of(ec * PEER_ROWS, PEER_ROWS)
    for which, src_ref in enumerate((th_ref, e0_ref)):
        for h in range(PEER_HEADS):
            group = src_ref[h, pl.ds(first_key, PEER_ROWS), :]
            for il in range(PEER_ROWS):
                rows_ref[which, h, il] = jnp.broadcast_to(group[il:il + 1], (8, group.shape[1]))

    n_gate = (PEER_TOK // N_KEYS) * (N_KEYS // PEER_JW)
    n_split = 4

    def matmul_piece(k, act_next, w_prev):
        if k < n_split:
            r = slice(k * (acc_ref.shape[0] // n_split), (k + 1) * (acc_ref.shape[0] // n_split))
            acc_ref[r, :] += jnp.dot(vt_ref[r, :], w_prev[...], preferred_element_type=F32)
        else:
            k -= n_split
            r = slice(k * (PEER_EXP // n_split), (k + 1) * (PEER_EXP // n_split))
            act_next[r, :] = lax.dot_general(u_next_ref[r, :], h_ref[...], NT_DIMS,
                                             preferred_element_type=F32)

    def step(act_cur, act_next, w_cur, w_prev):
        for piece in range(n_gate):
            if piece % (n_gate // (2 * n_split)) == 0:
                matmul_piece(piece // (n_gate // (2 * n_split)), act_next, w_prev)
            _peer_gate_piece(piece, s1_ref, e1_ref, rows_ref, act_cur, w_cur)

    @pl.when(ec % 2 == 0)
    def _():
        step(act_a, act_b, w_a, w_b)

    @pl.when(ec % 2 == 1)
    def _():
        step(act_b, act_a, w_b, w_a)

    @pl.when(ec == n_chunks - 1)
    def _():
        w_last = w_b
        acc = acc_ref[...] + jnp.dot(vt_last_ref[...], w_last[...], preferred_element_type=F32)
        o_ref[...] = acc.T


def _peer(h2, q, sub_keys, u_tab, v_tab):
    n, d = h2.shape
    n_exp = u_tab.shape[0]
    n_chunks = n_exp // PEER_EXP
    assert n_chunks % 2 == 0
    nq = q.shape[1]
    heads_tile = pltpu.VMEM((PEER_HEADS, N_KEYS, PEER_TOK), F32)
    act_tile = pltpu.VMEM((PEER_EXP, PEER_TOK), F32)
    w_tile = pltpu.VMEM((PEER_EXP, PEER_TOK), BF16)
    u16 = u_tab.astype(BF16)
    vt = v_tab.astype(BF16).T
    return pl.pallas_call(
        _peer_kernel,
        grid=(n // PEER_TOK, n_chunks),
        in_specs=[
            pl.BlockSpec((PEER_TOK, d), lambda t, e: (t, 0)),
            pl.BlockSpec((PEER_TOK, nq), lambda t, e: (t, 0)),
            pl.BlockSpec((PEER_HEADS, 2, N_KEYS, N_KEYS), lambda t, e: (0, 0, 0, 0)),
            pl.BlockSpec((PEER_EXP, d), lambda t, e: (0, 0)),
            pl.BlockSpec((PEER_EXP, d), lambda t, e: (jnp.minimum(e + 1, n_chunks - 1), 0)),
            pl.BlockSpec((d, PEER_EXP), lambda t, e: (0, jnp.maximum(e - 1, 0))),
            pl.BlockSpec((d, PEER_EXP), lambda t, e: (0, n_chunks - 1)),
        ],
        out_specs=pl.BlockSpec((PEER_TOK, d), lambda t, e: (t, 0)),
        out_shape=jax.ShapeDtypeStruct((n, d), F32),
        scratch_shapes=[heads_tile, heads_tile, heads_tile, heads_tile,
                        pltpu.VMEM((2, 24, PEER_TOK), F32),
                        pltpu.VMEM((2, PEER_HEADS, PEER_ROWS, 8, PEER_TOK), F32),
                        act_tile, act_tile, w_tile, w_tile,
                        pltpu.VMEM((d, PEER_TOK), F32)],
        compiler_params=_cparams(("parallel", "arbitrary")),
    )(h2, q, sub_keys.astype(BF16), u16, u16, vt, vt)


def _final_kernel(x_ref, p_ref, g2_ref, g_ref, o_ref):
    x = x_ref[0] + g2_ref[0] * p_ref[0]
    o_ref[0] = x * lax.rsqrt(jnp.mean(x * x, axis=-1, keepdims=True) + EPS) * g_ref[...]


def _final_norm(x, peer, mod, final_g, n_ctx):
    nb, t, d = x.shape
    skip = n_ctx // TOK_BLOCK
    blk = pl.BlockSpec((1, TOK_BLOCK, d), lambda b, i: (b, i + skip, 0))
    return pl.pallas_call(
        _final_kernel,
        grid=(nb, (t - n_ctx) // TOK_BLOCK),
        in_specs=[blk, blk,
                  pl.BlockSpec((1, 1, d), lambda b, i: (b * 6 + 5, 0, 0)),
                  pl.BlockSpec((1, d), lambda b, i: (0, 0))],
        out_specs=pl.BlockSpec((1, TOK_BLOCK, d), lambda b, i: (b, i, 0)),
        out_shape=jax.ShapeDtypeStruct((nb, t - n_ctx, d), F32),
        compiler_params=_cparams(("parallel", "parallel")),
    )(x, peer, mod, final_g.reshape(1, d))


def _rope_tables(n_ctx, n_tok, half):
    lane = jnp.arange(HEAD_W)
    part_w = 2 * half
    inv = ROPE_BASE ** (-(lane % half).astype(F32) / half)
    use_col = (lane // part_w) % 2 == 1
    tok = jnp.arange(n_tok)
    pos = jnp.where(use_col[None, :], (tok % GRID_W)[:, None], (tok // GRID_W)[:, None]).astype(F32)
    ang = pos * inv[None, :]
    sign = jnp.where((lane % part_w) < half, -1.0, 1.0)
    cos = jnp.concatenate([jnp.ones((n_ctx, HEAD_W), F32), jnp.cos(ang)], axis=0)
    sin = jnp.concatenate([jnp.zeros((n_ctx, HEAD_W), F32), jnp.sin(ang) * sign[None, :]], axis=0)
    return cos, sin


def kernel(x, c, ctx, c_ctx, w_ada, b_ada, norm_g, w_in_even, gmlp_ws, gmlp_bs, gmlp_v_gain, diff_lambda, diff_sub_gain, w_in_odd, hgrn_lower_bounds, hgrn_gain, ret_log_decay, ret_gain, w_out, peer_wq, peer_sub_keys, peer_u, peer_v, final_g):
    nb, n_tok, d = x.shape
    n_ctx = ctx.shape[1]
    depth = w_ada.shape[0]
    assert d == D_MODEL and n_ctx == TOK_BLOCK and n_tok % TOK_BLOCK == 0 and nb < 8
    assert (nb * (n_ctx + n_tok)) % PEER_TOK == 0
    t = n_ctx + n_tok

    stream = jnp.concatenate([ctx, x], axis=1)
    cvec = jnp.zeros((8, d), F32).at[:nb].set(c).at[nb].set(c_ctx)
    mod_all = _ada_mod(cvec, w_ada, b_ada).reshape(depth, 8 * 6, 1, d)

    lb_all = jnp.cumsum(jax.nn.softmax(hgrn_lower_bounds.astype(F32), axis=0), axis=0)
    lb_all = lb_all - lb_all[0]
    cos_e, sin_e = _rope_tables(n_ctx, n_tok, 16)
    cos_o, sin_o = _rope_tables(n_ctx, n_tok, 32)

    peer_out = None
    prev_mod = None
    for i in range(depth):
        j = i // 2
        mod = mod_all[i]
        if i % 2 == 0:
            lam_init = 0.8 - 0.6 * math.exp(-0.3 * i)
            proj, stream = _proj(stream, peer_out, prev_mod, mod, norm_g[i, 0], w_in_even[j].astype(BF16),
                                 cos_e, sin_e, rope_slabs=tuple(range(0, 4)) + tuple(range(16, 20)),
                                 rope_shift=16, out_dtype=BF16)
            a = _gmlp(proj, gmlp_ws[j], gmlp_bs[j], gmlp_v_gain[j])
            b = _diff_attention(proj, diff_lambda[j].astype(F32), diff_sub_gain[j], lam_init)
        else:
            proj, stream = _proj(stream, peer_out, prev_mod, mod, norm_g[i, 0], w_in_odd[j].astype(BF16),
                                 cos_o, sin_o, rope_slabs=tuple(range(12, 16)) + tuple(range(28, 32)),
                                 rope_shift=32, out_dtype=F32)
            a, b = _odd_finish(_scans(proj, lb_all[i], ret_log_decay[j]), proj, hgrn_gain[j], ret_gain[j])
        stream, h2, q = _out_proj(stream, a, b, w_out[i], mod, norm_g[i, 1], peer_wq[i])
        peer_out = _peer(h2.reshape(nb * t, d), q.reshape(nb * t, -1), peer_sub_keys[i],
                         peer_u[i], peer_v[i]).reshape(nb, t, d)
        prev_mod = mod
    return _final_norm(stream, peer_out, prev_mod, final_g, n_ctx)
```

```python
import functools
import math

import jax
import jax.numpy as jnp
from jax import lax
from jax.experimental import pallas as pl
from jax.experimental.pallas import tpu as pltpu

F32 = jnp.float32
BF16 = jnp.bfloat16

D_MODEL = 1024
MIX_W = D_MODEL // 2
HEAD_W = 128
N_HEADS = MIX_W // HEAD_W
GRID_W = 64
GMLP_CHUNK = 128
SCAN_CHUNK = 64
HGRN_SUB = 16
HGRN_GROUP = 4
PEER_HEADS = 8
N_KEYS = 128
PK_TOPK = 16
ROPE_BASE = 10000.0
EPS = 1e-6
DIFF_SCALE = 64 ** -0.5
C_SCALE = HEAD_W ** -0.5
D_SCALE = HEAD_W ** -0.5

TOK_BLOCK = 256
ATTN_KV_TILE = 768
PEER_TOK = 512
PEER_EXP = 1024
PEER_ROWS = PEER_EXP // N_KEYS
PEER_JW = 32
VMEM_LIMIT = 56 * 1024 * 1024

NT_DIMS = (((1,), (1,)), ((), ()))
TN_DIMS = (((0,), (0,)), ((), ()))


def _cparams(sem):
    return pltpu.CompilerParams(dimension_semantics=sem, vmem_limit_bytes=VMEM_LIMIT)


def _gelu(x):
    return 0.5 * x * (1.0 + jnp.tanh(0.7978845608028654 * (x + 0.044715 * x * x * x)))


def _sigmoid(x):
    return 1.0 / (1.0 + jnp.exp(-x))


def _mod_row(which, n_batch):
    def index(b, t):
        return (jnp.where(t == 0, n_batch, b) * 6 + which, 0, 0)
    return index


def _ada_kernel(c_ref, w_ref, b_ref, o_ref):
    c = c_ref[...]
    a = (c * _sigmoid(c)).astype(BF16)
    o_ref[0] = jnp.dot(a, w_ref[0].astype(BF16), preferred_element_type=F32) + b_ref[0]


def _ada_mod(cvec, w_ada, b_ada):
    depth, d, n = w_ada.shape
    tn = 1536
    return pl.pallas_call(
        _ada_kernel,
        grid=(depth, n // tn),
        in_specs=[
            pl.BlockSpec((8, d), lambda i, j: (0, 0)),
            pl.BlockSpec((1, d, tn), lambda i, j: (i, 0, j)),
            pl.BlockSpec((1, 1, tn), lambda i, j: (i, 0, j)),
        ],
        out_specs=pl.BlockSpec((1, 8, tn), lambda i, j: (i, 0, j)),
        out_shape=jax.ShapeDtypeStruct((depth, 8, n), F32),
        compiler_params=_cparams(("parallel", "parallel")),
    )(cvec, w_ada, b_ada.reshape(depth, 1, n))


def _rope_slab(seg, cos, sin, shift):
    lane = lax.broadcasted_iota(jnp.int32, seg.shape, 1)
    first = (lane % (2 * shift)) < shift
    partner = jnp.where(first, pltpu.roll(seg, HEAD_W - shift, 1), pltpu.roll(seg, shift, 1))
    return seg * cos + partner * sin


def _proj_kernel(*refs, rope_slabs, rope_shift, add_peer):
    if add_peer:
        x_ref, p_ref, g2_ref, gn_ref, sh_ref, sc_ref, w_ref, cos_ref, sin_ref, o_ref, s_ref = refs
        x = x_ref[0] + g2_ref[0] * p_ref[0]
        s_ref[0] = x
    else:
        x_ref, gn_ref, sh_ref, sc_ref, w_ref, cos_ref, sin_ref, o_ref = refs
        x = x_ref[0]
    h = x * lax.rsqrt(jnp.mean(x * x, axis=-1, keepdims=True) + EPS) * gn_ref[...]
    h = h * (1.0 + sc_ref[0]) + sh_ref[0]
    acc = jnp.dot(h.astype(BF16), w_ref[...], preferred_element_type=F32)
    o_ref[0] = acc.astype(o_ref.dtype)
    cos = cos_ref[...]
    sin = sin_ref[...]
    for c in rope_slabs:
        seg = acc[:, c * HEAD_W:(c + 1) * HEAD_W]
        o_ref[0, :, c * HEAD_W:(c + 1) * HEAD_W] = _rope_slab(seg, cos, sin, rope_shift).astype(o_ref.dtype)


def _proj(x, peer, prev_mod, mod, gn, w, cos, sin, *, rope_slabs, rope_shift, out_dtype):
    nb, t, d = x.shape
    n = w.shape[1]
    blk = pl.BlockSpec((1, TOK_BLOCK, d), lambda b, i: (b, i, 0))
    vec = lambda which: pl.BlockSpec((1, 1, d), _mod_row(which, nb))
    in_specs = [blk]
    args = [x]
    if peer is not None:
        in_specs += [blk, vec(5)]
        args += [peer, prev_mod]
    in_specs += [
        pl.BlockSpec((1, d), lambda b, i: (0, 0)),
        vec(0), vec(1),
        pl.BlockSpec((d, n), lambda b, i: (0, 0)),
        pl.BlockSpec((TOK_BLOCK, HEAD_W), lambda b, i: (i, 0)),
        pl.BlockSpec((TOK_BLOCK, HEAD_W), lambda b, i: (i, 0)),
    ]
    args += [gn.reshape(1, d), mod, mod, w, cos, sin]
    out_specs = [pl.BlockSpec((1, TOK_BLOCK, n), lambda b, i: (b, i, 0))]
    out_shape = [jax.ShapeDtypeStruct((nb, t, n), out_dtype)]
    if peer is not None:
        out_specs.append(blk)
        out_shape.append(jax.ShapeDtypeStruct((nb, t, d), F32))
    res = pl.pallas_call(
        functools.partial(_proj_kernel, rope_slabs=rope_slabs, rope_shift=rope_shift,
                          add_peer=peer is not None),
        grid=(nb, t // TOK_BLOCK),
        in_specs=in_specs,
        out_specs=out_specs,
        out_shape=out_shape,
        compiler_params=_cparams(("parallel", "parallel")),
    )(*args)
    return (res[0], res[1]) if peer is not None else (res[0], x)


def _attn_kernel(lv_ref, q_ref, k_ref, v_ref, sg_ref, o_ref, s_a, s_b, *, n_kv, kv_tile, lam_init):
    qi = pl.program_id(2)
    q = q_ref[0]
    lane = lax.broadcasted_iota(jnp.int32, q.shape, 1)
    qs = q * jnp.asarray(DIFF_SCALE, q.dtype)
    zero = jnp.zeros_like(qs)
    q1 = jnp.where(lane < HEAD_W // 2, qs, zero)
    q2 = jnp.where(lane < HEAD_W // 2, zero, qs)
    tq = q.shape[0]
    qq = jnp.concatenate([q1, q2], axis=0)

    def scores(rows):
        return lax.dot_general(qq, k_ref[0, rows, :], NT_DIMS, preferred_element_type=F32)

    def update(s, rows, carry):
        m, l, a = carry
        v = v_ref[0, rows, :]
        m_new = jnp.maximum(m, jnp.max(s, axis=-1, keepdims=True))
        alpha = jnp.exp(m - m_new)
        p = jnp.exp(s - m_new)
        l_new = alpha * l + jnp.sum(p, axis=-1, keepdims=True)
        a_new = alpha * a + jnp.dot(p.astype(v.dtype), v, preferred_element_type=F32)
        return m_new, l_new, a_new

    def tile(c):
        return pl.ds(pl.multiple_of(c * kv_tile, kv_tile), kv_tile)

    def finish(carry):
        m, l, a = carry
        lv = lv_ref[...]
        lam = (jnp.exp(jnp.sum(lv[0:1] * lv[1:2], axis=-1, keepdims=True))
               - jnp.exp(jnp.sum(lv[2:3] * lv[3:4], axis=-1, keepdims=True)) + lam_init)
        o = a / l
        o = o[:tq] - lam * o[tq:]
        o = o * lax.rsqrt(jnp.mean(o * o, axis=-1, keepdims=True) + EPS) * sg_ref[...]
        o_ref[0] = (o * (1.0 - lam_init)).astype(o_ref.dtype)

    init = (jnp.full((2 * tq, 1), -jnp.inf, F32), jnp.zeros((2 * tq, 1), F32),
            jnp.zeros((2 * tq, HEAD_W), F32))

    @pl.when(qi == 0)
    def _():
        ctx_rows = pl.ds(0, TOK_BLOCK)
        finish(update(scores(ctx_rows), ctx_rows, init))

    @pl.when(qi > 0)
    def _():
        n_pairs = (n_kv - 1) // 2
        s_a[...] = scores(tile(0))

        def pair(i, carry):
            c = 2 * i
            s_b[...] = scores(tile(c + 1))
            carry = update(s_a[...], tile(c), carry)
            s_a[...] = scores(tile(c + 2))
            return update(s_b[...], tile(c + 1), carry)

        carry = lax.fori_loop(0, n_pairs, pair, init)
        last = n_kv - 1
        if last == 2 * n_pairs:
            carry = update(s_a[...], tile(last), carry)
        else:
            s_b[...] = scores(tile(last))
            carry = update(s_a[...], tile(last - 1), carry)
            carry = update(s_b[...], tile(last), carry)
        finish(carry)


def _diff_attention(proj, lam_vecs, sub_gain, lam_init):
    nb, t, _ = proj.shape
    n_q = t // TOK_BLOCK
    kv_tile = ATTN_KV_TILE if t % ATTN_KV_TILE == 0 else TOK_BLOCK
    return pl.pallas_call(
        functools.partial(_attn_kernel, n_kv=t // kv_tile, kv_tile=kv_tile, lam_init=lam_init),
        grid=(nb, N_HEADS, n_q),
        in_specs=[
            pl.BlockSpec((4, HEAD_W // 2), lambda b, h, i: (0, 0)),
            pl.BlockSpec((1, TOK_BLOCK, HEAD_W), lambda b, h, i: (b, i, 4 * N_HEADS + h)),
            pl.BlockSpec((1, t, HEAD_W), lambda b, h, i: (b, 0, h)),
            pl.BlockSpec((1, t, HEAD_W), lambda b, h, i: (b, 0, N_HEADS + h)),
            pl.BlockSpec((1, HEAD_W), lambda b, h, i: (0, 0)),
        ],
        out_specs=pl.BlockSpec((1, TOK_BLOCK, HEAD_W), lambda b, h, i: (b, i, h)),
        out_shape=jax.ShapeDtypeStruct((nb, t, MIX_W), BF16),
        scratch_shapes=[pltpu.VMEM((2 * TOK_BLOCK, kv_tile), F32)] * 2,
        compiler_params=_cparams(("parallel", "parallel", "arbitrary")),
    )(lam_vecs, proj, proj, proj, sub_gain.reshape(1, HEAD_W))


def _gmlp_kernel(u_ref, v_ref, ws_ref, bs_ref, vg_ref, o_ref):
    u = _gelu(u_ref[0].astype(F32))
    v = _gelu(v_ref[0].astype(F32))
    vc = v - jnp.mean(v, axis=-1, keepdims=True)
    vn = vc * lax.rsqrt(jnp.mean(vc * vc, axis=-1, keepdims=True) + EPS) * vg_ref[...]
    vn = vn.astype(BF16)
    for ch in range(TOK_BLOCK // GMLP_CHUNK):
        rows = slice(ch * GMLP_CHUNK, (ch + 1) * GMLP_CHUNK)
        for g in range(N_HEADS):
            cols = slice(g * HEAD_W, (g + 1) * HEAD_W)
            z = jnp.dot(ws_ref[g], vn[rows, cols], preferred_element_type=F32) + bs_ref[g]
            o_ref[0, rows, cols] = (u[rows, cols] * z).astype(o_ref.dtype)


def _gmlp(proj, ws, bs, v_gain):
    nb, t, _ = proj.shape
    bs_b = jnp.broadcast_to(bs[:, :, None], (N_HEADS, GMLP_CHUNK, HEAD_W)).astype(F32)
    return pl.pallas_call(
        _gmlp_kernel,
        grid=(nb, t // TOK_BLOCK),
        in_specs=[
            pl.BlockSpec((1, TOK_BLOCK, MIX_W), lambda b, i: (b, i, 2)),
            pl.BlockSpec((1, TOK_BLOCK, MIX_W), lambda b, i: (b, i, 3)),
            pl.BlockSpec((N_HEADS, GMLP_CHUNK, GMLP_CHUNK), lambda b, i: (0, 0, 0)),
            pl.BlockSpec((N_HEADS, GMLP_CHUNK, HEAD_W), lambda b, i: (0, 0, 0)),
            pl.BlockSpec((1, MIX_W), lambda b, i: (0, 0)),
        ],
        out_specs=pl.BlockSpec((1, TOK_BLOCK, MIX_W), lambda b, i: (b, i, 0)),
        out_shape=jax.ShapeDtypeStruct((nb, t, MIX_W), BF16),
        compiler_params=_cparams(("parallel", "parallel")),
    )(proj, proj, ws.astype(BF16), bs_b, v_gain.reshape(1, MIX_W))


def _scan_block_index(n_blk, reverse):
    if not reverse:
        return lambda i: i
    return lambda i: jnp.where(i == 0, 0, n_blk - i)


def _hgrn_chunk(q, k, v, logf, st_ref, reverse):
    c = q.shape[0]
    row = lax.broadcasted_iota(jnp.int32, (c, c), 0)
    col = lax.broadcasted_iota(jnp.int32, (c, c), 1)
    tri = ((col >= row) if reverse else (col <= row)).astype(F32)
    cum = jnp.dot(tri, logf, preferred_element_type=F32, precision=lax.Precision.HIGHEST)
    tot = jnp.sum(logf, axis=0, keepdims=True)
    st = st_ref[...]
    o_inter = lax.dot_general(q * jnp.exp(cum), st, NT_DIMS, preferred_element_type=F32)
    t8 = lax.broadcasted_iota(jnp.int32, (8, 1), 0)
    pieces = []
    for blk in range(c // HGRN_SUB):
        lo = blk * HGRN_SUB
        rows = slice(lo, lo + HGRN_SUB)
        o_blk = o_inter[rows]
        earlier = slice(lo + HGRN_SUB, c) if reverse else slice(0, lo)
        if earlier.stop > earlier.start:
            edge = lo + HGRN_SUB if reverse else lo - 1
            ref = cum[edge:edge + 1]
            qt = q[rows] * jnp.exp(cum[rows] - ref)
            kt = k[earlier] * jnp.exp(ref - cum[earlier])
            att = lax.dot_general(qt, kt, NT_DIMS, preferred_element_type=F32)
            o_blk = o_blk + jnp.dot(att, v[earlier], preferred_element_type=F32)
        halves = [o_blk[0:8], o_blk[8:16]]
        for sl in range(HGRN_SUB):
            s = lo + sl
            needed = [hh for hh in range(2) if ((8 * hh <= sl) if reverse else (8 * hh + 7 >= sl))]
            for hh in needed:
                r = slice(lo + 8 * hh, lo + 8 * hh + 8)
                w = q[r] * jnp.exp(jnp.minimum(cum[r] - cum[s:s + 1], 0.0)) * k[s:s + 1]
                a = jnp.sum(w, axis=-1, keepdims=True)
                t_loc = t8 + 8 * hh
                valid = (t_loc <= sl) if reverse else (t_loc >= sl)
                halves[hh] = halves[hh] + jnp.where(valid, a, 0.0) * v[s:s + 1]
        pieces += halves
    kd = k * jnp.exp(tot - cum)
    st_ref[...] = st * jnp.exp(tot) + lax.dot_general(v, kd, TN_DIMS, preferred_element_type=F32)
    return pieces


def _hgrn_kernel(fl_ref, i_ref, hq_ref, lb_ref, o_ref, st_ref, *, reverse):
    @pl.when(pl.program_id(2) == 0)
    def _():
        st_ref[...] = jnp.zeros_like(st_ref)

    n_chunks = TOK_BLOCK // SCAN_CHUNK

    def chunk(j, carry):
        ci = n_chunks - 1 - j if reverse else j
        base = pl.multiple_of(ci * SCAN_CHUNK, SCAN_CHUNK)
        rows = pl.ds(base, SCAN_CHUNK)
        for g in range(HGRN_GROUP):
            cols = slice(g * HEAD_W, (g + 1) * HEAD_W)
            lb = lb_ref[g]
            f = lb + (1.0 - lb) * _sigmoid(fl_ref[0, rows, cols])
            hq = hq_ref[0, rows, cols]
            q = hq * _sigmoid(hq) * C_SCALE
            pieces = _hgrn_chunk(q, 1.0 - f, i_ref[0, rows, cols], jnp.log(f), st_ref.at[g], reverse)
            for p, piece in enumerate(pieces):
                o_ref[0, pl.ds(pl.multiple_of(base + 8 * p, 8), 8), cols] = piece
        return carry

    lax.fori_loop(0, n_chunks, chunk, 0)


def _ret_kernel(rk_ref, rv_ref, rq_ref, dec_ref, o_ref, st_ref, *, reverse):
    @pl.when(pl.program_id(1) == 0)
    def _():
        st_ref[...] = jnp.zeros_like(st_ref)

    c = TOK_BLOCK
    row = lax.broadcasted_iota(jnp.int32, (c, c), 0)
    col = lax.broadcasted_iota(jnp.int32, (c, c), 1)
    lag = ((col - row) if reverse else (row - col)).astype(F32)
    t_idx = lax.broadcasted_iota(jnp.int32, (c, 1), 0)
    steps = ((c - t_idx) if reverse else (t_idx + 1)).astype(F32)
    for h in range(N_HEADS):
        cols = slice(h * HEAD_W, (h + 1) * HEAD_W)
        lg = -jnp.exp(dec_ref[h])
        decay = jnp.where(lag >= 0, jnp.exp(lag * lg[:, 0:1]), 0.0)
        cum = steps * lg
        tot = c * lg
        q = rq_ref[0, :, cols]
        k = rk_ref[0, :, cols] * D_SCALE
        v = rv_ref[0, :, cols]
        st = st_ref[h]
        o = lax.dot_general(q * jnp.exp(cum), st, NT_DIMS, preferred_element_type=F32)
        att = lax.dot_general(q, k, NT_DIMS, preferred_element_type=F32) * decay
        o_ref[0, :, cols] = o + jnp.dot(att, v, preferred_element_type=F32)
        kd = k * jnp.exp(tot - cum)
        st_ref[h] = st * jnp.exp(tot) + lax.dot_general(v, kd, TN_DIMS, preferred_element_type=F32)


def _scans(proj, lb, ret_decay):
    nb, t, _ = proj.shape
    n_blk = t // TOK_BLOCK
    out_shape = jax.ShapeDtypeStruct((nb, t, MIX_W), F32)
    dec = jnp.broadcast_to(ret_decay.reshape(2, N_HEADS, 1, 1), (2, N_HEADS, 1, HEAD_W)).astype(F32)
    group_w = HGRN_GROUP * HEAD_W
    outs = []
    for d in (0, 1):
        blk = _scan_block_index(n_blk, d == 1)
        col = lambda base, blk=blk: pl.BlockSpec(
            (1, TOK_BLOCK, group_w), lambda b, g, i: (b, blk(i), base * (MIX_W // group_w) + g))
        outs.append(pl.pallas_call(
            functools.partial(_hgrn_kernel, reverse=d == 1),
            grid=(nb, N_HEADS // HGRN_GROUP, n_blk),
            in_specs=[col(d), col(2), col(5),
                      pl.BlockSpec((HGRN_GROUP, 1, HEAD_W), lambda b, g, i: (g, 0, 0))],
            out_specs=pl.BlockSpec((1, TOK_BLOCK, group_w), lambda b, g, i, blk=blk: (b, blk(i), g)),
            out_shape=out_shape,
            scratch_shapes=[pltpu.VMEM((HGRN_GROUP, HEAD_W, HEAD_W), F32)],
            compiler_params=_cparams(("parallel", "parallel", "arbitrary")),
        )(proj, proj, proj, lb.reshape(N_HEADS, 1, HEAD_W)))
    for d in (0, 1):
        blk = _scan_block_index(n_blk, d == 1)
        col = lambda base, blk=blk: pl.BlockSpec((1, TOK_BLOCK, MIX_W), lambda b, i: (b, blk(i), base))
        outs.append(pl.pallas_call(
            functools.partial(_ret_kernel, reverse=d == 1),
            grid=(nb, n_blk),
            in_specs=[col(3), col(4), col(7),
                      pl.BlockSpec((N_HEADS, 1, HEAD_W), lambda b, i, d=d: (d, 0, 0))],
            out_specs=pl.BlockSpec((1, TOK_BLOCK, MIX_W), lambda b, i, blk=blk: (b, blk(i), 0)),
            out_shape=out_shape,
            scratch_shapes=[pltpu.VMEM((N_HEADS, HEAD_W, HEAD_W), F32)],
            compiler_params=_cparams(("parallel", "arbitrary")),
        )(proj, proj, proj, dec.reshape(2 * N_HEADS, 1, HEAD_W)))
    return outs


def _odd_finish_kernel(ohf_ref, ohb_ref, orf_ref, orb_ref, hg_ref, rg_ref, hgain_ref, rgain_ref,
                       a_ref, b_ref):
    oh = ohf_ref[0] + ohb_ref[0]
    orr = orf_ref[0] + orb_ref[0]
    gh = oh * _sigmoid(hg_ref[0])
    rg = rg_ref[0]
    for h in range(N_HEADS):
        cols = slice(h * HEAD_W, (h + 1) * HEAD_W)
        x = gh[:, cols]
        a_ref[0, :, cols] = (x * lax.rsqrt(jnp.mean(x * x, axis=-1, keepdims=True) + EPS)
                             * hgain_ref[...]).astype(a_ref.dtype)
        y = orr[:, cols]
        yc = y - jnp.mean(y, axis=-1, keepdims=True)
        yn = yc * lax.rsqrt(jnp.mean(yc * yc, axis=-1, keepdims=True) + EPS) * rgain_ref[...]
        r = rg[:, cols]
        b_ref[0, :, cols] = (yn * (r * _sigmoid(r))).astype(b_ref.dtype)


def _odd_finish(scan_outs, proj, hg_gain, ret_gain):
    nb, t, _ = scan_outs[0].shape
    o_spec = pl.BlockSpec((1, TOK_BLOCK, MIX_W), lambda b, i: (b, i, 0))
    return pl.pallas_call(
        _odd_finish_kernel,
        grid=(nb, t // TOK_BLOCK),
        in_specs=[o_spec, o_spec, o_spec, o_spec,
                  pl.BlockSpec((1, TOK_BLOCK, MIX_W), lambda b, i: (b, i, 6)),
                  pl.BlockSpec((1, TOK_BLOCK, MIX_W), lambda b, i: (b, i, 8)),
                  pl.BlockSpec((1, HEAD_W), lambda b, i: (0, 0)),
                  pl.BlockSpec((1, HEAD_W), lambda b, i: (0, 0))],
        out_specs=[o_spec, o_spec],
        out_shape=[jax.ShapeDtypeStruct((nb, t, MIX_W), BF16)] * 2,
        compiler_params=_cparams(("parallel", "parallel")),
    )(*scan_outs, proj, proj, hg_gain.reshape(1, HEAD_W), ret_gain.reshape(1, HEAD_W))


def _out_kernel(x_ref, a_ref, b_ref, wa_ref, wb_ref, g1_ref, gn_ref, sh_ref, sc_ref, wq_ref,
                xo_ref, h_ref, q_ref):
    y = (jnp.dot(a_ref[0], wa_ref[...], preferred_element_type=F32)
         + jnp.dot(b_ref[0], wb_ref[...], preferred_element_type=F32))
    x = x_ref[0] + g1_ref[0] * y
    xo_ref[0] = x
    h = x * lax.rsqrt(jnp.mean(x * x, axis=-1, keepdims=True) + EPS) * gn_ref[...]
    h = (h * (1.0 + sc_ref[0]) + sh_ref[0]).astype(BF16)
    h_ref[0] = h
    q_ref[0] = jnp.dot(h, wq_ref[...], preferred_element_type=F32).astype(q_ref.dtype)


def _out_proj(x, a, b, w_out, mod, gn, wq):
    nb, t, d = x.shape
    nq = wq.shape[1]
    blk = lambda w: pl.BlockSpec((1, TOK_BLOCK, w), lambda bb, i: (bb, i, 0))
    vec = lambda which: pl.BlockSpec((1, 1, d), _mod_row(which, nb))
    full = lambda r, c: pl.BlockSpec((r, c), lambda bb, i: (0, 0))
    w16 = w_out.astype(BF16)
    return pl.pallas_call(
        _out_kernel,
        grid=(nb, t // TOK_BLOCK),
        in_specs=[blk(d), blk(MIX_W), blk(MIX_W), full(MIX_W, d), full(MIX_W, d),
                  vec(2), full(1, d), vec(3), vec(4), full(d, nq)],
        out_specs=[blk(d), blk(d), blk(nq)],
        out_shape=[jax.ShapeDtypeStruct((nb, t, d), F32),
                   jax.ShapeDtypeStruct((nb, t, d), BF16),
                   jax.ShapeDtypeStruct((nb, t, nq), BF16)],
        compiler_params=_cparams(("parallel", "parallel")),
    )(x, a, b, w16[:MIX_W], w16[MIX_W:], mod, gn.reshape(1, d), mod, mod, wq.astype(BF16))


def _sorting_network(n):
    pairs = []
    p = 1
    while p < n:
        k = p
        while k >= 1:
            for j in range(k % p, n - k, 2 * k):
                for i in range(min(k, n - j - k)):
                    if (i + j) // (2 * p) == (i + j + k) // (2 * p):
                        pairs.append((i + j, i + j + k))
            k //= 2
        p *= 2
    return pairs


def _top_values(work, n):
    n_rows, n_cols = work.shape
    n_groups = 1
    while n_groups * 8 < n_rows:
        n_groups *= 2
    neg = jnp.full((8, n_cols), -jnp.inf, F32)
    lists = [work[8 * g:8 * g + 8] if 8 * g < n_rows else neg for g in range(n_groups)]
    for a, b in _sorting_network(n_groups):
        lists[a], lists[b] = jnp.maximum(lists[a], lists[b]), jnp.minimum(lists[a], lists[b])
    rows = []
    for r in range(n):
        m = jnp.max(lists[0], axis=0, keepdims=True)
        rows.append(m)
        taken = lists[0] == m
        for k in range(min(n_groups, n - 1 - r)):
            lists[k] = jnp.where(taken, lists[k + 1] if k + 1 < n_groups else neg, lists[k])
    return rows


def _peer_scores(q_ref, keys_ref, s1_ref, e1_ref, th_ref, e0_ref, top_ref):
    n_top = PK_TOPK + 1
    top_ref[...] = jnp.full(top_ref.shape, -jnp.inf, F32)
    for h in range(PEER_HEADS):
        scores, tops = [], []
        for side in range(2):
            slab = (2 * h + side) * N_KEYS
            s = lax.dot_general(keys_ref[h, side], q_ref[:, slab:slab + N_KEYS], NT_DIMS,
                                preferred_element_type=F32)
            vals = _top_values(s, n_top)
            for r in range(n_top):
                top_ref[side, r:r + 1, :] = vals[r]
            scores.append(s)
            tops.append(vals)
        t0, t1 = tops
        cands = [t0[0] + top_ref[1]]
        cands += [t0[a] + top_ref[1, 0:8, :] for a in range(1, 8)]
        cands += [top_ref[0, 8:, :] + t1[0]]
        best = _top_values(jnp.concatenate(cands, axis=0), n_top)
        z = jnp.zeros_like(best[0])
        for r in range(PK_TOPK):
            z = z + jnp.exp(best[r] - best[0])
        tau = 0.5 * (best[PK_TOPK - 1] + best[PK_TOPK])
        th_ref[h] = tau - scores[0]
        e0_ref[h] = jnp.exp(scores[0] - t0[0])
        s1_ref[h] = scores[1]
        e1_ref[h] = jnp.exp(scores[1] - t1[0]) / z


def _peer_gate_piece(piece, s1_ref, e1_ref, rows_ref, act_ref, w_ref):
    ct, jq = divmod(piece, N_KEYS // PEER_JW)
    lanes = slice(ct * N_KEYS, (ct + 1) * N_KEYS)
    jrows = slice(jq * PEER_JW, (jq + 1) * PEER_JW)
    accs = [None] * PEER_ROWS
    for h in range(PEER_HEADS):
        s1 = s1_ref[h, jrows, lanes]
        e1 = e1_ref[h, jrows, lanes]
        for il in range(PEER_ROWS):
            th = jnp.concatenate([rows_ref[0, h, il, :, lanes]] * (PEER_JW // 8), axis=0)
            e0 = jnp.concatenate([rows_ref[1, h, il, :, lanes]] * (PEER_JW // 8), axis=0)
            gate = jnp.where(s1 >= th, e1 * e0, 0.0)
            accs[il] = gate if h == 0 else accs[il] + gate
    for il in range(PEER_ROWS):
        rows = slice(il * N_KEYS + jq * PEER_JW, il * N_KEYS + (jq + 1) * PEER_JW)
        x = act_ref[rows, lanes]
        inner = x * (0.7978845608028654 + (0.7978845608028654 * 0.044715) * (x * x))
        w_ref[rows, lanes] = ((accs[il] * (0.5 * x)) * (1.0 + jnp.tanh(inner))).astype(w_ref.dtype)


def _peer_kernel(h_ref, q_ref, keys_ref, u_first_ref, u_next_ref, vt_ref, vt_last_ref, o_ref,
                 s1_ref, e1_ref, th_ref, e0_ref, top_ref, rows_ref, act_a, act_b, w_a, w_b, acc_ref):
    ec = pl.program_id(1)
    n_chunks = pl.num_programs(1)

    @pl.when(ec == 0)
    def _():
        _peer_scores(q_ref, keys_ref, s1_ref, e1_ref, th_ref, e0_ref, top_ref)
        act_a[...] = lax.dot_general(u_first_ref[...], h_ref[...], NT_DIMS, preferred_element_type=F32)
        w_b[...] = jnp.zeros_like(w_b)
        acc_ref[...] = jnp.zeros_like(acc_ref)

    first_key = pl.multiple_of(ec * PEER_ROWS, PEER_ROWS)
    for which, src_ref in enumerate((th_ref, e0_ref)):
        for h in range(PEER_HEADS):
            group = src_ref[h, pl.ds(first_key, PEER_ROWS), :]
            for il in range(PEER_ROWS):
                rows_ref[which, h, il] = jnp.broadcast_to(group[il:il + 1], (8, group.shape[1]))

    n_gate = (PEER_TOK // N_KEYS) * (N_KEYS // PEER_JW)
    n_split = 4

    def matmul_piece(k, act_next, w_prev):
        if k < n_split:
            r = slice(k * (acc_ref.shape[0] // n_split), (k + 1) * (acc_ref.shape[0] // n_split))
            acc_ref[r, :] += jnp.dot(vt_ref[r, :], w_prev[...], preferred_element_type=F32)
        else:
            k -= n_split
            r = slice(k * (PEER_EXP // n_split), (k + 1) * (PEER_EXP // n_split))
            act_next[r, :] = lax.dot_general(u_next_ref[r, :], h_ref[...], NT_DIMS,
                                             preferred_element_type=F32)

    def step(act_cur, act_next, w_cur, w_prev):
        every = n_gate // (2 * n_split)
        for piece in range(n_gate):
            if piece % every == 0:
                matmul_piece(piece // every, act_next, w_prev)
            _peer_gate_piece(piece, s1_ref, e1_ref, rows_ref, act_cur, w_cur)

    @pl.when(ec % 2 == 0)
    def _():
        step(act_a, act_b, w_a, w_b)

    @pl.when(ec % 2 == 1)
    def _():
        step(act_b, act_a, w_b, w_a)

    @pl.when(ec == n_chunks - 1)
    def _():
        w_last = w_b
        acc = acc_ref[...] + jnp.dot(vt_last_ref[...], w_last[...], preferred_element_type=F32)
        o_ref[...] = acc.T


def _peer(h2, q, sub_keys, u_tab, v_tab):
    n, d = h2.shape
    n_exp = u_tab.shape[0]
    n_chunks = n_exp // PEER_EXP
    assert n_chunks % 2 == 0
    nq = q.shape[1]
    heads_tile = pltpu.VMEM((PEER_HEADS, N_KEYS, PEER_TOK), F32)
    act_tile = pltpu.VMEM((PEER_EXP, PEER_TOK), F32)
    w_tile = pltpu.VMEM((PEER_EXP, PEER_TOK), BF16)
    u16 = u_tab.astype(BF16)
    vt = v_tab.astype(BF16).T
    return pl.pallas_call(
        _peer_kernel,
        grid=(n // PEER_TOK, n_chunks),
        in_specs=[
            pl.BlockSpec((PEER_TOK, d), lambda t, e: (t, 0)),
            pl.BlockSpec((PEER_TOK, nq), lambda t, e: (t, 0)),
            pl.BlockSpec((PEER_HEADS, 2, N_KEYS, N_KEYS), lambda t, e: (0, 0, 0, 0)),
            pl.BlockSpec((PEER_EXP, d), lambda t, e: (0, 0)),
            pl.BlockSpec((PEER_EXP, d), lambda t, e: (jnp.minimum(e + 1, n_chunks - 1), 0)),
            pl.BlockSpec((d, PEER_EXP), lambda t, e: (0, jnp.maximum(e - 1, 0))),
            pl.BlockSpec((d, PEER_EXP), lambda t, e: (0, n_chunks - 1)),
        ],
        out_specs=pl.BlockSpec((PEER_TOK, d), lambda t, e: (t, 0)),
        out_shape=jax.ShapeDtypeStruct((n, d), F32),
        scratch_shapes=[heads_tile, heads_tile, heads_tile, heads_tile,
                        pltpu.VMEM((2, 24, PEER_TOK), F32),
                        pltpu.VMEM((2, PEER_HEADS, PEER_ROWS, 8, PEER_TOK), F32),
                        act_tile, act_tile, w_tile, w_tile,
                        pltpu.VMEM((d, PEER_TOK), F32)],
        compiler_params=_cparams(("parallel", "arbitrary")),
    )(h2, q, sub_keys.astype(BF16), u16, u16, vt, vt)


def _final_kernel(x_ref, p_ref, g2_ref, g_ref, o_ref):
    x = x_ref[0] + g2_ref[0] * p_ref[0]
    o_ref[0] = x * lax.rsqrt(jnp.mean(x * x, axis=-1, keepdims=True) + EPS) * g_ref[...]


def _final_norm(x, peer, mod, final_g, n_ctx):
    nb, t, d = x.shape
    skip = n_ctx // TOK_BLOCK
    blk = pl.BlockSpec((1, TOK_BLOCK, d), lambda b, i: (b, i + skip, 0))
    return pl.pallas_call(
        _final_kernel,
        grid=(nb, (t - n_ctx) // TOK_BLOCK),
        in_specs=[blk, blk,
                  pl.BlockSpec((1, 1, d), lambda b, i: (b * 6 + 5, 0, 0)),
                  pl.BlockSpec((1, d), lambda b, i: (0, 0))],
        out_specs=pl.BlockSpec((1, TOK_BLOCK, d), lambda b, i: (b, i, 0)),
        out_shape=jax.ShapeDtypeStruct((nb, t - n_ctx, d), F32),
        compiler_params=_cparams(("parallel", "parallel")),
    )(x, peer, mod, final_g.reshape(1, d))


def _rope_tables(n_ctx, n_tok, half):
    lane = jnp.arange(HEAD_W)
    part_w = 2 * half
    inv = ROPE_BASE ** (-(lane % half).astype(F32) / half)
    use_col = (lane // part_w) % 2 == 1
    tok = jnp.arange(n_tok)
    pos = jnp.where(use_col[None, :], (tok % GRID_W)[:, None], (tok // GRID_W)[:, None]).astype(F32)
    ang = pos * inv[None, :]
    sign = jnp.where((lane % part_w) < half, -1.0, 1.0)
    cos = jnp.concatenate([jnp.ones((n_ctx, HEAD_W), F32), jnp.cos(ang)], axis=0)
    sin = jnp.concatenate([jnp.zeros((n_ctx, HEAD_W), F32), jnp.sin(ang) * sign[None, :]], axis=0)
    return cos, sin


def kernel(x, c, ctx, c_ctx, w_ada, b_ada, norm_g, w_in_even, gmlp_ws, gmlp_bs, gmlp_v_gain, diff_lambda, diff_sub_gain, w_in_odd, hgrn_lower_bounds, hgrn_gain, ret_log_decay, ret_gain, w_out, peer_wq, peer_sub_keys, peer_u, peer_v, final_g):
    nb, n_tok, d = x.shape
    n_ctx = ctx.shape[1]
    depth = w_ada.shape[0]
    assert d == D_MODEL and n_ctx == TOK_BLOCK and n_tok % TOK_BLOCK == 0 and nb < 8
    assert (nb * (n_ctx + n_tok)) % PEER_TOK == 0
    t = n_ctx + n_tok

    stream = jnp.concatenate([ctx, x], axis=1)
    cvec = jnp.zeros((8, d), F32).at[:nb].set(c).at[nb].set(c_ctx)
    mod_all = _ada_mod(cvec, w_ada, b_ada).reshape(depth, 8 * 6, 1, d)

    lb_all = jnp.cumsum(jax.nn.softmax(hgrn_lower_bounds.astype(F32), axis=0), axis=0)
    lb_all = lb_all - lb_all[0]
    cos_e, sin_e = _rope_tables(n_ctx, n_tok, 16)
    cos_o, sin_o = _rope_tables(n_ctx, n_tok, 32)

    peer_out = None
    prev_mod = None
    for i in range(depth):
        j = i // 2
        mod = mod_all[i]
        if i % 2 == 0:
            lam_init = 0.8 - 0.6 * math.exp(-0.3 * i)
            proj, stream = _proj(stream, peer_out, prev_mod, mod, norm_g[i, 0], w_in_even[j].astype(BF16),
                                 cos_e, sin_e, rope_slabs=tuple(range(0, 4)) + tuple(range(16, 20)),
                                 rope_shift=16, out_dtype=BF16)
            a = _gmlp(proj, gmlp_ws[j], gmlp_bs[j], gmlp_v_gain[j])
            b = _diff_attention(proj, diff_lambda[j].astype(F32), diff_sub_gain[j], lam_init)
        else:
            proj, stream = _proj(stream, peer_out, prev_mod, mod, norm_g[i, 0], w_in_odd[j].astype(BF16),
                                 cos_o, sin_o, rope_slabs=tuple(range(12, 16)) + tuple(range(28, 32)),
                                 rope_shift=32, out_dtype=F32)
            a, b = _odd_finish(_scans(proj, lb_all[i], ret_log_decay[j]), proj, hgrn_gain[j], ret_gain[j])
        stream, h2, q = _out_proj(stream, a, b, w_out[i], mod, norm_g[i, 1], peer_wq[i])
        peer_out = _peer(h2.reshape(nb * t, d), q.reshape(nb * t, -1), peer_sub_keys[i],
                         peer_u[i], peer_v[i]).reshape(nb, t, d)
        prev_mod = mod
    return _final_norm(stream, peer_out, prev_mod, final_g, n_ctx)
```
